```python
import math
import jax, jax.numpy as jnp
from jax import lax
import numpy as np

D_MODEL = 1024
BATCH = 16
SEQ = 2048
DEPTH = 1
DEC_BATCH = 32
DEC_SEQ = 64
PAST_LEN = 1024

CHUNK = 64
POOL_WIDTH = D_MODEL // 2
POOL_WINDOWS = (2, 4, 8, 16)
N_POOL_GROUPS = len(POOL_WINDOWS)
POOL_GROUP = POOL_WIDTH // N_POOL_GROUPS
POOL_HIST = max(POOL_WINDOWS) - 1
N_HEADS = 4
HEAD_DIM = D_MODEL // (4 * N_HEADS)
V_DIM = 2 * HEAD_DIM
QK_WIDTH = N_HEADS * 2 * HEAD_DIM
ATTN_WIDTH = N_HEADS * V_DIM
PROJ_WIDTH = POOL_WIDTH + 2 * QK_WIDTH + ATTN_WIDTH
MIX_WIDTH = POOL_WIDTH + ATTN_WIDTH
ATTN_SCALE = HEAD_DIM ** -0.5
Q_BLOCK = 128
NEG_INF = -1e30
N_GROUPS = 4
EXPERTS_PER_GROUP = 4
N_EXPERTS = N_GROUPS * EXPERTS_PER_GROUP
TOP_K = 2
D_EXPERT = 256
EPS = 1e-6

kernel_name = "hymba_pool_diffattn_hiermoe_stream_step"


def rmsnorm(x, g):
    xf = x.astype(jnp.float32)
    y = xf * lax.rsqrt(jnp.mean(xf * xf, axis=-1, keepdims=True) + EPS)
    return (y * g.astype(jnp.float32)).astype(x.dtype)


def lambda_init_for(layer):
    return 0.8 - 0.6 * math.exp(-0.3 * layer)


def multiscale_pool(u, hist, start_pos, w_pool, pool_scale):
    B, T, C = u.shape
    ext = jnp.concatenate([hist.astype(jnp.float32), u.astype(jnp.float32)], axis=1)
    csum = jnp.concatenate([jnp.zeros((B, 1, C), jnp.float32), jnp.cumsum(ext, axis=1)], axis=1)
    pos = start_pos + jnp.arange(T, dtype=jnp.int32)
    uf = u.astype(jnp.float32)
    diffs = []
    for g, w in enumerate(POOL_WINDOWS):
        lo_c, hi_c = g * POOL_GROUP, (g + 1) * POOL_GROUP
        hi = csum[:, POOL_HIST + 1:POOL_HIST + 1 + T, lo_c:hi_c]
        lo = csum[:, POOL_HIST + 1 - w:POOL_HIST + 1 - w + T, lo_c:hi_c]
        cnt = jnp.minimum(w, pos + 1).astype(jnp.float32)[None, :, None]
        diffs.append((hi - lo) / cnt - uf[..., lo_c:hi_c])
    d = jnp.stack(diffs, axis=2)
    y = jnp.einsum('btgc,gce->btge', d, w_pool.astype(jnp.float32)).reshape(B, T, C)
    return (y * pool_scale.astype(jnp.float32)).astype(u.dtype)


def diff_attend(q, k, v, q_pos, k_pos, lam):
    s = jnp.einsum('bqhcd,bkhcd->bhcqk', q, k) * ATTN_SCALE
    mask = (k_pos[None, :] // CHUNK) <= (q_pos[:, None] // CHUNK)
    s = jnp.where(mask, s, NEG_INF)
    p = jax.nn.softmax(s, axis=-1)
    a = p[:, :, 0] - lam * p[:, :, 1]
    return jnp.einsum('bhqk,bkhe->bqhe', a, v)


def layer_forward(x, pool_hist, k_hist, v_hist, start_pos, layer,
                  g_mix, w_in, w_pool, pool_scale, lam_q1, lam_k1, lam_q2, lam_k2, g_subln, w_out,
                  g_ffn, w_group, b_group, w_erouter, b_erouter, w_gate, w_up, w_down):
    B, T, _ = x.shape
    h = rmsnorm(x, g_mix)
    proj = h @ w_in
    u = proj[..., :POOL_WIDTH]
    q = proj[..., POOL_WIDTH:POOL_WIDTH + QK_WIDTH]
    k = proj[..., POOL_WIDTH + QK_WIDTH:POOL_WIDTH + 2 * QK_WIDTH]
    v = proj[..., POOL_WIDTH + 2 * QK_WIDTH:]
    k_new = k.reshape(B, T, N_HEADS, 2 * HEAD_DIM)
    v_new = v.reshape(B, T, N_HEADS, V_DIM)

    pool_out = multiscale_pool(u, pool_hist, start_pos, w_pool, pool_scale)
    new_pool = jnp.concatenate([pool_hist, u], axis=1)[:, -POOL_HIST:]

    if k_hist is None:
        k_all, v_all = k_new, v_new
    else:
        k_all = jnp.concatenate([k_hist.astype(k_new.dtype), k_new], axis=1)
        v_all = jnp.concatenate([v_hist.astype(v_new.dtype), v_new], axis=1)
    Tk = k_all.shape[1]
    k_pos = jnp.arange(Tk, dtype=jnp.int32)
    q_pos = start_pos + jnp.arange(T, dtype=jnp.int32)
    qf = q.reshape(B, T, N_HEADS, 2, HEAD_DIM).astype(jnp.float32)
    kf = k_all.reshape(B, Tk, N_HEADS, 2, HEAD_DIM).astype(jnp.float32)
    vf = v_all.astype(jnp.float32)
    li = lambda_init_for(layer)
    lam = (jnp.exp(jnp.sum(lam_q1.astype(jnp.float32) * lam_k1.astype(jnp.float32)))
           - jnp.exp(jnp.sum(lam_q2.astype(jnp.float32) * lam_k2.astype(jnp.float32))) + li)
    if T > Q_BLOCK:
        nb = T // Q_BLOCK
        qb = qf.reshape(B, nb, Q_BLOCK, N_HEADS, 2, HEAD_DIM).transpose(1, 0, 2, 3, 4, 5)
        pb = q_pos.reshape(nb, Q_BLOCK)
        ob = lax.map(lambda a: diff_attend(a[0], kf, vf, a[1], k_pos, lam), (qb, pb))
        o = ob.transpose(1, 0, 2, 3, 4).reshape(B, T, N_HEADS, V_DIM)
    else:
        o = diff_attend(qf, kf, vf, q_pos, k_pos, lam)
    o = o * lax.rsqrt(jnp.mean(o * o, axis=-1, keepdims=True) + EPS) * g_subln.astype(jnp.float32) * (1.0 - li)
    attn_out = o.reshape(B, T, ATTN_WIDTH).astype(x.dtype)

    x = x + jnp.concatenate([pool_out, attn_out], axis=-1) @ w_out

    h2 = rmsnorm(x, g_ffn)
    hf = h2.astype(jnp.float32)
    g_logits = hf @ w_group.astype(jnp.float32) + b_group.astype(jnp.float32)
    g_prob = jax.nn.softmax(g_logits, axis=-1)
    g_idx = jnp.argmax(g_logits, axis=-1)
    g_gate = jnp.take_along_axis(g_prob, g_idx[..., None], axis=-1)
    e_logits = (hf @ w_erouter.astype(jnp.float32) + b_erouter.astype(jnp.float32)).reshape(B, T, N_GROUPS, EXPERTS_PER_GROUP)
    e_sel = jnp.take_along_axis(e_logits, g_idx[..., None, None], axis=2)[..., 0, :]
    top_v, top_i = lax.top_k(e_sel, TOP_K)
    top_w = jax.nn.softmax(top_v, axis=-1) * g_gate
    expert_id = g_idx[..., None] * EXPERTS_PER_GROUP + top_i
    combine = jnp.sum(jax.nn.one_hot(expert_id, N_EXPERTS, dtype=jnp.float32) * top_w[..., None], axis=2)
    moe = jnp.zeros((B, T, D_MODEL), jnp.float32)
    for e in range(N_EXPERTS):
        he = jax.nn.silu(h2 @ w_gate[e]) * (h2 @ w_up[e])
        moe = moe + combine[..., e:e + 1] * (he @ w_down[e]).astype(jnp.float32)
    x = x + moe.astype(x.dtype)
    return x, k_new, v_new, new_pool


def setup_inputs(seed: int = 0) -> dict:
    key = jax.random.key(seed)
    ks = jax.random.split(key, 24)
    f32 = jnp.float32
    nrm = lambda k, shape, s: jax.random.normal(k, shape, f32) * s
    return {
        "x_prompt": nrm(ks[0], (BATCH, SEQ, D_MODEL), 1.0),
        "x_sample": nrm(ks[1], (DEC_BATCH, DEC_SEQ, D_MODEL), 1.0),
        "cache_k": nrm(ks[2], (DEPTH, DEC_BATCH, PAST_LEN, N_HEADS, 2 * HEAD_DIM), 1.0),
        "cache_v": nrm(ks[3], (DEPTH, DEC_BATCH, PAST_LEN, N_HEADS, V_DIM), 1.0),
        "state_pool": nrm(ks[4], (DEPTH, DEC_BATCH, POOL_HIST, POOL_WIDTH), 1.0),
        "g_mix": 1.0 + nrm(ks[5], (DEPTH, D_MODEL), 0.05),
        "w_in": nrm(ks[6], (DEPTH, D_MODEL, PROJ_WIDTH), D_MODEL ** -0.5),
        "w_pool": nrm(ks[7], (DEPTH, N_POOL_GROUPS, POOL_GROUP, POOL_GROUP), POOL_GROUP ** -0.5),
        "pool_scale": 1.0 + nrm(ks[8], (DEPTH, POOL_WIDTH), 0.1),
        "lam_q1": nrm(ks[9], (DEPTH, HEAD_DIM), 0.1),
        "lam_k1": nrm(ks[10], (DEPTH, HEAD_DIM), 0.1),
        "lam_q2": nrm(ks[11], (DEPTH, HEAD_DIM), 0.1),
        "lam_k2": nrm(ks[12], (DEPTH, HEAD_DIM), 0.1),
        "g_subln": 1.0 + nrm(ks[13], (DEPTH, V_DIM), 0.05),
        "w_out": nrm(ks[14], (DEPTH, MIX_WIDTH, D_MODEL), MIX_WIDTH ** -0.5),
        "g_ffn": 1.0 + nrm(ks[15], (DEPTH, D_MODEL), 0.05),
        "w_group": nrm(ks[16], (DEPTH, D_MODEL, N_GROUPS), D_MODEL ** -0.5),
        "b_group": nrm(ks[17], (DEPTH, N_GROUPS), 0.01),
        "w_erouter": nrm(ks[18], (DEPTH, D_MODEL, N_EXPERTS), D_MODEL ** -0.5),
        "b_erouter": nrm(ks[19], (DEPTH, N_EXPERTS), 0.01),
        "w_gate": nrm(ks[20], (DEPTH, N_EXPERTS, D_MODEL, D_EXPERT), D_MODEL ** -0.5),
        "w_up": nrm(ks[21], (DEPTH, N_EXPERTS, D_MODEL, D_EXPERT), D_MODEL ** -0.5),
        "w_down": nrm(ks[22], (DEPTH, N_EXPERTS, D_EXPERT, D_MODEL), D_EXPERT ** -0.5),
        "g_final": 1.0 + nrm(ks[23], (D_MODEL,), 0.05),
    }


def reference(x_prompt, x_sample, cache_k, cache_v, state_pool,
              g_mix, w_in, w_pool, pool_scale, lam_q1, lam_k1, lam_q2, lam_k2, g_subln, w_out,
              g_ffn, w_group, b_group, w_erouter, b_erouter, w_gate, w_up, w_down, g_final):
    xp, xs = x_prompt, x_sample
    kp_l, vp_l, pp_l, ks_l, vs_l, ps_l = [], [], [], [], [], []
    for l in range(DEPTH):
        w = (g_mix[l], w_in[l], w_pool[l], pool_scale[l], lam_q1[l], lam_k1[l], lam_q2[l], lam_k2[l],
             g_subln[l], w_out[l], g_ffn[l], w_group[l], b_group[l], w_erouter[l], b_erouter[l],
             w_gate[l], w_up[l], w_down[l])
        zero_hist = jnp.zeros((xp.shape[0], POOL_HIST, POOL_WIDTH), xp.dtype)
        xp, kp, vp, pp = layer_forward(xp, zero_hist, None, None, 0, l, *w)
        xs, kn, vn, pn = layer_forward(xs, state_pool[l].astype(xs.dtype), cache_k[l], cache_v[l], PAST_LEN, l, *w)
        kp_l.append(kp); vp_l.append(vp); pp_l.append(pp)
        ks_l.append(kn); vs_l.append(vn); ps_l.append(pn)
    y_prompt = rmsnorm(xp, g_final)
    y_sample = rmsnorm(xs, g_final)
    k_prompt = jnp.stack(kp_l); v_prompt = jnp.stack(vp_l); pool_prompt = jnp.stack(pp_l)
    k_sample = jnp.stack(ks_l); v_sample = jnp.stack(vs_l); pool_sample = jnp.stack(ps_l)
    return (y_prompt, y_sample, k_prompt, v_prompt, pool_prompt, k_sample, v_sample, pool_sample)
```

```python
import functools
import math

import jax
import jax.numpy as jnp
from jax import lax
from jax.experimental import pallas as pl
from jax.experimental.pallas import tpu as pltpu

F32 = jnp.float32
BF16 = jnp.bfloat16

D_MODEL = 1024
CHUNK = 64
POOL_WIDTH = 512
POOL_WINDOWS = (2, 4, 8, 16)
POOL_GROUP = 128
POOL_HIST = 15
HIST_ROWS = 16
N_HEADS = 4
HEAD_DIM = 64
HEAD_WIDTH = 128
QK_WIDTH = 512
PROJ_WIDTH = 2048
ATTN_SCALE = HEAD_DIM ** -0.5
NEG_INF = -1e30
N_GROUPS = 4
EXPERTS_PER_GROUP = 4
N_EXPERTS = 16
D_EXPERT = 256
EPS = 1e-6
LANES = 128
ROUTER_LANE0 = N_GROUPS
VMEM_LIMIT = 48 * 1024 * 1024


def _lambda_init(layer):
    return 0.8 - 0.6 * math.exp(-0.3 * layer)


def _inproj_kernel(x_ref, hist_ref, gmix_ref, win_ref, wpool_ref, pscale_ref,
                   k_ref, v_ref, qb_ref, kb_ref, vb_ref, pool_ref, state_ref,
                   ext_ref, *, tt, start_pos):
    t = pl.program_id(1)
    x = x_ref[0]
    ms = jnp.mean(x * x, axis=-1, keepdims=True)
    h = (x * lax.rsqrt(ms + EPS) * gmix_ref[...]).astype(BF16)
    proj = jnp.dot(h, win_ref[...], preferred_element_type=F32)
    u = proj[:, :POOL_WIDTH]
    q = proj[:, POOL_WIDTH:POOL_WIDTH + QK_WIDTH]
    k = proj[:, POOL_WIDTH + QK_WIDTH:POOL_WIDTH + 2 * QK_WIDTH]
    v = proj[:, POOL_WIDTH + 2 * QK_WIDTH:]
    k_ref[0] = k
    v_ref[0] = v
    qb_ref[0] = (q * ATTN_SCALE).astype(BF16)
    kb_ref[0] = k.astype(BF16)
    vb_ref[0] = v.astype(BF16)

    @pl.when(t == 0)
    def _():
        ext_ref[0:HIST_ROWS] = hist_ref[0]

    @pl.when(t > 0)
    def _():
        ext_ref[0:HIST_ROWS] = ext_ref[tt:tt + HIST_ROWS]

    ext_ref[HIST_ROWS:HIST_ROWS + tt] = u

    pos = start_pos + t * tt + lax.broadcasted_iota(jnp.int32, (tt, 1), 0)
    outs = []
    for g, w in enumerate(POOL_WINDOWS):
        lo = g * POOL_GROUP
        ug = u[:, lo:lo + POOL_GROUP]
        acc = ug
        for j in range(1, w):
            acc = acc + ext_ref[HIST_ROWS - j:HIST_ROWS - j + tt, lo:lo + POOL_GROUP]
        inv_cnt = 1.0 / jnp.minimum(w, pos + 1).astype(F32)
        d = acc * inv_cnt - ug
        outs.append(jnp.dot(d.astype(BF16), wpool_ref[g], preferred_element_type=F32))
    y = jnp.concatenate(outs, axis=-1) * pscale_ref[...]
    pool_ref[0] = y.astype(BF16)

    @pl.when(t == pl.num_programs(1) - 1)
    def _():
        state_ref[0] = ext_ref[tt + 1:tt + HIST_ROWS]


def _inproj(x, hist16, g_mix, w_in, w_pool, pool_scale, *, tt, start_pos):
    B, T, _ = x.shape
    nt = T // tt
    row = lambda b, t: (b, t, 0)
    const2 = lambda b, t: (0, 0)
    wide = lambda dt: jax.ShapeDtypeStruct((B, T, QK_WIDTH), dt)
    return pl.pallas_call(
        functools.partial(_inproj_kernel, tt=tt, start_pos=start_pos),
        grid=(B, nt),
        in_specs=[
            pl.BlockSpec((1, tt, D_MODEL), row),
            pl.BlockSpec((1, HIST_ROWS, POOL_WIDTH), lambda b, t: (b, 0, 0)),
            pl.BlockSpec((1, D_MODEL), const2),
            pl.BlockSpec((D_MODEL, PROJ_WIDTH), const2),
            pl.BlockSpec((len(POOL_WINDOWS), POOL_GROUP, POOL_GROUP), lambda b, t: (0, 0, 0)),
            pl.BlockSpec((1, POOL_WIDTH), const2),
        ],
        out_specs=[
            pl.BlockSpec((1, tt, QK_WIDTH), row),
            pl.BlockSpec((1, tt, QK_WIDTH), row),
            pl.BlockSpec((1, tt, QK_WIDTH), row),
            pl.BlockSpec((1, tt, QK_WIDTH), row),
            pl.BlockSpec((1, tt, QK_WIDTH), row),
            pl.BlockSpec((1, tt, POOL_WIDTH), row),
            pl.BlockSpec((1, POOL_HIST, POOL_WIDTH), lambda b, t: (b, 0, 0)),
        ],
        out_shape=[wide(F32), wide(F32), wide(BF16), wide(BF16), wide(BF16), wide(BF16),
                   jax.ShapeDtypeStruct((B, POOL_HIST, POOL_WIDTH), F32)],
        scratch_shapes=[pltpu.VMEM((HIST_ROWS + tt, POOL_WIDTH), F32)],
        compiler_params=pltpu.CompilerParams(
            dimension_semantics=("parallel", "arbitrary"), vmem_limit_bytes=VMEM_LIMIT),
        name="inproj_pool",
    )(x, hist16, g_mix, w_in, w_pool, pool_scale)


def _lambda_value(lq1_ref, lk1_ref, lq2_ref, lk2_ref, li):
    s1 = jnp.sum(lq1_ref[...] * lk1_ref[...], axis=-1, keepdims=True)
    s2 = jnp.sum(lq2_ref[...] * lk2_ref[...], axis=-1, keepdims=True)
    return jnp.exp(s1) - jnp.exp(s2) + li


def _split_halves(q):
    lane = lax.broadcasted_iota(jnp.int32, (1, HEAD_WIDTH), 1)
    zero = jnp.zeros_like(q)
    return jnp.where(lane < HEAD_DIM, q, zero), jnp.where(lane >= HEAD_DIM, q, zero)


def _scores(qh, kblk):
    return lax.dot_general(qh, kblk, (((1,), (1,)), ((), ())), preferred_element_type=F32)


def _subln(o, gsub_ref, li):
    return o * lax.rsqrt(jnp.mean(o * o, axis=-1, keepdims=True) + EPS) * gsub_ref[...] * (1.0 - li)


def _attn_prompt_kernel(lq1_ref, lk1_ref, lq2_ref, lk2_ref, gsub_ref, q_ref, k_ref, v_ref, o_ref,
                        *, tq, li):
    qi = pl.program_id(2)
    q0, q1 = _split_halves(q_ref[0])

    def step(j, carry, masked):
        off = pl.multiple_of(j * tq, tq)
        kblk = k_ref[0, pl.ds(off, tq), :]
        vblk = v_ref[0, pl.ds(off, tq), :]
        new = []
        for qh, (m, l, acc) in zip((q0, q1), carry):
            s = _scores(qh, kblk)
            if masked:
                rq = lax.broadcasted_iota(jnp.int32, (tq, tq), 0) // CHUNK
                ck = lax.broadcasted_iota(jnp.int32, (tq, tq), 1) // CHUNK
                s = jnp.where(ck <= rq, s, NEG_INF)
            m_new = jnp.maximum(m, jnp.max(s, axis=-1, keepdims=True))
            alpha = jnp.exp(m - m_new)
            p = jnp.exp(s - m_new)
            l = alpha * l + jnp.sum(p, axis=-1, keepdims=True)
            acc = alpha * acc + jnp.dot(p.astype(BF16), vblk, preferred_element_type=F32)
            new.append((m_new, l, acc))
        return tuple(new)

    init_one = (jnp.full((tq, 1), NEG_INF, F32), jnp.zeros((tq, 1), F32),
                jnp.zeros((tq, HEAD_WIDTH), F32))
    carry = lax.fori_loop(0, qi, lambda j, c: step(j, c, False), (init_one, init_one))
    (_, l0, a0), (_, l1, a1) = step(qi, carry, True)
    lam = _lambda_value(lq1_ref, lk1_ref, lq2_ref, lk2_ref, li)
    o = a0 / l0 - lam * (a1 / l1)
    o_ref[0] = _subln(o, gsub_ref, li).astype(BF16)


def _attn_prompt(lams, g_subln, qb, kb, vb, *, tq, li):
    B, T, _ = qb.shape
    c2 = lambda b, h, i: (0, 0)
    small = pl.BlockSpec((1, HEAD_DIM), c2)
    return pl.pallas_call(
        functools.partial(_attn_prompt_kernel, tq=tq, li=li),
        grid=(B, N_HEADS, T // tq),
        in_specs=[small, small, small, small,
                  pl.BlockSpec((1, HEAD_WIDTH), c2),
                  pl.BlockSpec((1, tq, HEAD_WIDTH), lambda b, h, i: (b, i, h)),
                  pl.BlockSpec((1, T, HEAD_WIDTH), lambda b, h, i: (b, 0, h)),
                  pl.BlockSpec((1, T, HEAD_WIDTH), lambda b, h, i: (b, 0, h))],
        out_specs=pl.BlockSpec((1, tq, HEAD_WIDTH), lambda b, h, i: (b, i, h)),
        out_shape=jax.ShapeDtypeStruct((B, T, QK_WIDTH), BF16),
        compiler_params=pltpu.CompilerParams(
            dimension_semantics=("parallel", "parallel", "arbitrary"), vmem_limit_bytes=VMEM_LIMIT),
        name="attn_prompt",
    )(*lams, g_subln, qb, kb, vb)


def _attn_sample_kernel(lq1_ref, lk1_ref, lq2_ref, lk2_ref, gsub_ref, q_ref, kc_ref, vc_ref,
                        kn_ref, vn_ref, o_ref, *, li):
    q0, q1 = _split_halves(q_ref[0])
    kc = kc_ref[0].astype(BF16)
    vc = vc_ref[0].astype(BF16)
    kn = kn_ref[0]
    vn = vn_ref[0]
    outs = []
    for qh in (q0, q1):
        sc = _scores(qh, kc)
        sn = _scores(qh, kn)
        m = jnp.maximum(jnp.max(sc, axis=-1, keepdims=True), jnp.max(sn, axis=-1, keepdims=True))
        pc = jnp.exp(sc - m)
        pn = jnp.exp(sn - m)
        l = jnp.sum(pc, axis=-1, keepdims=True) + jnp.sum(pn, axis=-1, keepdims=True)
        acc = (jnp.dot(pc.astype(BF16), vc, preferred_element_type=F32)
               + jnp.dot(pn.astype(BF16), vn, preferred_element_type=F32))
        outs.append(acc / l)
    lam = _lambda_value(lq1_ref, lk1_ref, lq2_ref, lk2_ref, li)
    o = outs[0] - lam * outs[1]
    o_ref[0] = _subln(o, gsub_ref, li).astype(BF16)


def _attn_sample(lams, g_subln, qb, cache_k, cache_v, kb, vb, *, li):
    B, T, _ = qb.shape
    P = cache_k.shape[1]
    c2 = lambda b, h: (0, 0)
    small = pl.BlockSpec((1, HEAD_DIM), c2)
    head = lambda rows: pl.BlockSpec((1, rows, HEAD_WIDTH), lambda b, h: (b, 0, h))
    return pl.pallas_call(
        functools.partial(_attn_sample_kernel, li=li),
        grid=(B, N_HEADS),
        in_specs=[small, small, small, small, pl.BlockSpec((1, HEAD_WIDTH), c2),
                  head(T), head(P), head(P), head(T), head(T)],
        out_specs=head(T),
        out_shape=jax.ShapeDtypeStruct((B, T, QK_WIDTH), BF16),
        compiler_params=pltpu.CompilerParams(
            dimension_semantics=("parallel", "parallel"), vmem_limit_bytes=VMEM_LIMIT),
        name="attn_sample",
    )(*lams, g_subln, qb, cache_k, cache_v, kb, vb)


def _lane_first_argmax(vals, lane):
    vmax = jnp.max(vals, axis=-1, keepdims=True)
    idx = jnp.min(jnp.where(vals == vmax, lane, LANES), axis=-1, keepdims=True)
    return vmax, idx


def _outproj_router_kernel(x_ref, pool_ref, attn_ref, wout_ref, gffn_ref, wr_hi_ref, wr_lo_ref, br_ref,
                           x1_ref, h2_ref, comb_ref):
    mix = (jnp.dot(pool_ref[...], wout_ref[0:POOL_WIDTH, :], preferred_element_type=F32)
           + jnp.dot(attn_ref[...], wout_ref[POOL_WIDTH:, :], preferred_element_type=F32))
    x1 = x_ref[...] + mix
    x1_ref[...] = x1
    ms = jnp.mean(x1 * x1, axis=-1, keepdims=True)
    h2 = x1 * lax.rsqrt(ms + EPS) * gffn_ref[...]
    h_hi = h2.astype(BF16)
    h2_ref[...] = h_hi
    h_lo = (h2 - h_hi.astype(F32)).astype(BF16)
    logits = (jnp.dot(h_hi, wr_hi_ref[...], preferred_element_type=F32)
              + jnp.dot(h_hi, wr_lo_ref[...], preferred_element_type=F32)
              + jnp.dot(h_lo, wr_hi_ref[...], preferred_element_type=F32)
              + br_ref[...])
    lane = lax.broadcasted_iota(jnp.int32, logits.shape, 1)
    neg = jnp.float32(-jnp.inf)
    gl = jnp.where(lane < N_GROUPS, logits, neg)
    gmax, g_idx = _lane_first_argmax(gl, lane)
    g_gate = 1.0 / jnp.sum(jnp.exp(gl - gmax), axis=-1, keepdims=True)
    e_lo = ROUTER_LANE0 + EXPERTS_PER_GROUP * g_idx
    el = jnp.where((lane >= e_lo) & (lane < e_lo + EXPERTS_PER_GROUP), logits, neg)
    v1, i1 = _lane_first_argmax(el, lane)
    v2, i2 = _lane_first_argmax(jnp.where(lane == i1, neg, el), lane)
    e21 = jnp.exp(v2 - v1)
    w1 = g_gate / (1.0 + e21)
    w2 = w1 * e21
    comb_ref[...] = jnp.where(lane == i1, w1, 0.0) + jnp.where(lane == i2, w2, 0.0)


def _outproj_router(x2d, pool2d, attn2d, w_out, g_ffn, wr_hi, wr_lo, br, *, tm):
    N = x2d.shape[0]
    row = lambda i: (i, 0)
    c2 = lambda i: (0, 0)
    return pl.pallas_call(
        _outproj_router_kernel,
        grid=(N // tm,),
        in_specs=[pl.BlockSpec((tm, D_MODEL), row),
                  pl.BlockSpec((tm, POOL_WIDTH), row),
                  pl.BlockSpec((tm, QK_WIDTH), row),
                  pl.BlockSpec((D_MODEL, D_MODEL), c2),
                  pl.BlockSpec((1, D_MODEL), c2),
                  pl.BlockSpec((D_MODEL, LANES), c2),
                  pl.BlockSpec((D_MODEL, LANES), c2),
                  pl.BlockSpec((1, LANES), c2)],
        out_specs=[pl.BlockSpec((tm, D_MODEL), row),
                   pl.BlockSpec((tm, D_MODEL), row),
                   pl.BlockSpec((tm, LANES), row)],
        out_shape=[jax.ShapeDtypeStruct((N, D_MODEL), F32),
                   jax.ShapeDtypeStruct((N, D_MODEL), BF16),
                   jax.ShapeDtypeStruct((N, LANES), F32)],
        compiler_params=pltpu.CompilerParams(
            dimension_semantics=("parallel",), vmem_limit_bytes=VMEM_LIMIT),
        name="outproj_router",
    )(x2d, pool2d, attn2d, w_out, g_ffn, wr_hi, wr_lo, br)


def _moe_kernel(h2_ref, comb_ref, x1_ref, wg_ref, wu_ref, wd_ref, gfin_ref, y_ref, acc_ref):
    e = pl.program_id(1)

    @pl.when(e == 0)
    def _():
        acc_ref[...] = jnp.zeros_like(acc_ref)

    comb = comb_ref[...]
    lane = lax.broadcasted_iota(jnp.int32, comb.shape, 1)
    c = jnp.sum(jnp.where(lane == ROUTER_LANE0 + e, comb, 0.0), axis=-1, keepdims=True)
    h = h2_ref[...]
    a = jnp.dot(h, wg_ref[0], preferred_element_type=F32)
    b = jnp.dot(h, wu_ref[0], preferred_element_type=F32)
    he = (a * jax.nn.sigmoid(a) * b).astype(BF16)
    acc_ref[...] += c * jnp.dot(he, wd_ref[0], preferred_element_type=F32)

    @pl.when(e == pl.num_programs(1) - 1)
    def _():
        x2 = x1_ref[...] + acc_ref[...]
        ms = jnp.mean(x2 * x2, axis=-1, keepdims=True)
        y_ref[...] = x2 * lax.rsqrt(ms + EPS) * gfin_ref[...]


def _moe(h2, comb, x1, w_gate, w_up, w_down, g_final, *, tm):
    N = h2.shape[0]
    row = lambda i, e: (i, 0)
    return pl.pallas_call(
        _moe_kernel,
        grid=(N // tm, N_EXPERTS),
        in_specs=[pl.BlockSpec((tm, D_MODEL), row),
                  pl.BlockSpec((tm, LANES), row),
                  pl.BlockSpec((tm, D_MODEL), row),
                  pl.BlockSpec((1, D_MODEL, D_EXPERT), lambda i, e: (e, 0, 0)),
                  pl.BlockSpec((1, D_MODEL, D_EXPERT), lambda i, e: (e, 0, 0)),
                  pl.BlockSpec((1, D_EXPERT, D_MODEL), lambda i, e: (e, 0, 0)),
                  pl.BlockSpec((1, D_MODEL), lambda i, e: (0, 0))],
        out_specs=pl.BlockSpec((tm, D_MODEL), row),
        out_shape=jax.ShapeDtypeStruct((N, D_MODEL), F32),
        scratch_shapes=[pltpu.VMEM((tm, D_MODEL), F32)],
        compiler_params=pltpu.CompilerParams(
            dimension_semantics=("parallel", "arbitrary"), vmem_limit_bytes=VMEM_LIMIT),
        name="moe",
    )(h2, comb, x1, w_gate, w_up, w_down, g_final)


def _layer(x, hist16, cache, start_pos, layer, w, g_final, *, tt, tm):
    B, T, _ = x.shape
    li = _lambda_init(layer)
    k, v, qb, kb, vb, pool, state = _inproj(x, hist16, w["g_mix"], w["w_in"], w["w_pool"],
                                            w["pool_scale"], tt=tt, start_pos=start_pos)
    if cache is None:
        attn = _attn_prompt(w["lams"], w["g_subln"], qb, kb, vb, tq=256, li=li)
    else:
        attn = _attn_sample(w["lams"], w["g_subln"], qb, cache[0], cache[1], kb, vb, li=li)
    N = B * T
    x1, h2, comb = _outproj_router(x.reshape(N, D_MODEL), pool.reshape(N, POOL_WIDTH),
                                   attn.reshape(N, QK_WIDTH), w["w_out"], w["g_ffn"],
                                   w["wr_hi"], w["wr_lo"], w["br"], tm=tm)
    y = _moe(h2, comb, x1, w["w_gate"], w["w_up"], w["w_down"], g_final, tm=tm)
    return (y.reshape(B, T, D_MODEL), k.reshape(B, T, N_HEADS, HEAD_WIDTH),
            v.reshape(B, T, N_HEADS, HEAD_WIDTH), state)


def kernel(x_prompt, x_sample, cache_k, cache_v, state_pool, g_mix, w_in, w_pool, pool_scale, lam_q1, lam_k1, lam_q2, lam_k2, g_subln, w_out, g_ffn, w_group, b_group, w_erouter, b_erouter, w_gate, w_up, w_down, g_final):
    depth = g_mix.shape[0]
    assert depth == 1, "the final norm is fused into the layer, so exactly one layer is supported"
    l = 0
    wr = jnp.concatenate([w_group[l], w_erouter[l]], axis=1)
    wr = jnp.pad(wr, ((0, 0), (0, LANES - wr.shape[1])))
    wr_hi = wr.astype(BF16)
    wr_lo = (wr - wr_hi.astype(F32)).astype(BF16)
    br = jnp.pad(jnp.concatenate([b_group[l], b_erouter[l]]), (0, LANES - N_GROUPS - N_EXPERTS))
    w = dict(
        g_mix=g_mix[l][None], w_in=w_in[l].astype(BF16), w_pool=w_pool[l].astype(BF16),
        pool_scale=pool_scale[l][None],
        lams=(lam_q1[l][None], lam_k1[l][None], lam_q2[l][None], lam_k2[l][None]),
        g_subln=g_subln[l][None], w_out=w_out[l].astype(BF16), g_ffn=g_ffn[l][None],
        wr_hi=wr_hi, wr_lo=wr_lo, br=br[None],
        w_gate=w_gate[l].astype(BF16), w_up=w_up[l].astype(BF16), w_down=w_down[l].astype(BF16),
    )
    gfin = g_final[None]
    Bp = x_prompt.shape[0]
    Bs, _, _ = x_sample.shape
    P = cache_k.shape[2]
    zero_hist = jnp.zeros((Bp, HIST_ROWS, POOL_WIDTH), F32)
    samp_hist = jnp.pad(state_pool[l], ((0, 0), (HIST_ROWS - POOL_HIST, 0), (0, 0)))
    cache = (cache_k[l].reshape(Bs, P, QK_WIDTH), cache_v[l].reshape(Bs, P, QK_WIDTH))

    yp, kp, vp, pp = _layer(x_prompt, zero_hist, None, 0, l, w, gfin, tt=512, tm=512)
    ys, kn, vn, pn = _layer(x_sample, samp_hist, cache, P, l, w, gfin, tt=64, tm=512)
    return (yp, ys, kp[None], vp[None], pp[None], kn[None], vn[None], pn[None])
```

```python
import functools
import math

import jax
import jax.numpy as jnp
from jax import lax
from jax.experimental import pallas as pl
from jax.experimental.pallas import tpu as pltpu

F32 = jnp.float32
BF16 = jnp.bfloat16

D_MODEL = 1024
CHUNK = 64
POOL_WIDTH = 512
POOL_WINDOWS = (2, 4, 8, 16)
POOL_GROUP = 128
POOL_HIST = 15
HIST_ROWS = 16
N_HEADS = 4
HEAD_DIM = 64
HEAD_WIDTH = 128
QK_WIDTH = 512
PROJ_WIDTH = 2048
ATTN_SCALE = HEAD_DIM ** -0.5
NEG_INF = -1e30
N_GROUPS = 4
EXPERTS_PER_GROUP = 4
N_EXPERTS = 16
D_EXPERT = 256
EPS = 1e-6
LANES = 128
KV_BLOCK = 256
Q_SUB = 128
ROUTER_LANE0 = N_GROUPS
VMEM_LIMIT = 48 * 1024 * 1024


def _lambda_init(layer):
    return 0.8 - 0.6 * math.exp(-0.3 * layer)


def _inproj_kernel(x_ref, hist_ref, gmix_ref, win_ref, wvt_ref, wpool_ref, pscale_ref,
                   k_ref, v_ref, qb_ref, kb_ref, vb_ref, pool_ref, state_ref,
                   ext_ref, *, tt, start_pos, transposed_v):
    t = pl.program_id(1)
    x = x_ref[0]
    ms = jnp.mean(x * x, axis=-1, keepdims=True)
    h = (x * lax.rsqrt(ms + EPS) * gmix_ref[...]).astype(BF16)
    proj = jnp.dot(h, win_ref[...], preferred_element_type=F32)
    u = proj[:, :POOL_WIDTH]
    q = proj[:, POOL_WIDTH:POOL_WIDTH + QK_WIDTH]
    k = proj[:, POOL_WIDTH + QK_WIDTH:POOL_WIDTH + 2 * QK_WIDTH]
    v = proj[:, POOL_WIDTH + 2 * QK_WIDTH:]
    k_ref[0] = k
    v_ref[0] = v
    qb_ref[0] = (q * ATTN_SCALE).astype(BF16)
    kb_ref[0] = k.astype(BF16)
    if transposed_v:
        vt = lax.dot_general(wvt_ref[...], h, (((1,), (1,)), ((), ())),
                             preferred_element_type=F32).astype(BF16)
        for i in range(tt // KV_BLOCK):
            vb_ref[0, i] = vt[:, i * KV_BLOCK:(i + 1) * KV_BLOCK]
    else:
        vb_ref[0] = v.astype(BF16)

    @pl.when(t == 0)
    def _():
        ext_ref[0:HIST_ROWS] = hist_ref[0]

    @pl.when(t > 0)
    def _():
        ext_ref[0:HIST_ROWS] = ext_ref[tt:tt + HIST_ROWS]

    ext_ref[HIST_ROWS:HIST_ROWS + tt] = u

    pos = start_pos + t * tt + lax.broadcasted_iota(jnp.int32, (tt, 1), 0)
    outs = []
    for g, w in enumerate(POOL_WINDOWS):
        lo = g * POOL_GROUP
        ug = u[:, lo:lo + POOL_GROUP]
        acc = ug
        for j in range(1, w):
            acc = acc + ext_ref[HIST_ROWS - j:HIST_ROWS - j + tt, lo:lo + POOL_GROUP]
        inv_cnt = 1.0 / jnp.minimum(w, pos + 1).astype(F32)
        d = acc * inv_cnt - ug
        outs.append(jnp.dot(d.astype(BF16), wpool_ref[g], preferred_element_type=F32))
    y = jnp.concatenate(outs, axis=-1) * pscale_ref[...]
    pool_ref[0] = y.astype(BF16)

    @pl.when(t == pl.num_programs(1) - 1)
    def _():
        state_ref[0] = ext_ref[tt + 1:tt + HIST_ROWS]


def _inproj(x, hist16, g_mix, w_in, w_vt, w_pool, pool_scale, *, tt, start_pos, transposed_v):
    B, T, _ = x.shape
    nt = T // tt
    row = lambda b, t: (b, t, 0)
    const2 = lambda b, t: (0, 0)
    wide = lambda dt: jax.ShapeDtypeStruct((B, T, QK_WIDTH), dt)
    if transposed_v:
        vb_spec = pl.BlockSpec((1, tt // KV_BLOCK, QK_WIDTH, KV_BLOCK), lambda b, t: (b, t, 0, 0))
        vb_shape = jax.ShapeDtypeStruct((B, T // KV_BLOCK, QK_WIDTH, KV_BLOCK), BF16)
    else:
        vb_spec = pl.BlockSpec((1, tt, QK_WIDTH), row)
        vb_shape = wide(BF16)
    return pl.pallas_call(
        functools.partial(_inproj_kernel, tt=tt, start_pos=start_pos, transposed_v=transposed_v),
        grid=(B, nt),
        in_specs=[
            pl.BlockSpec((1, tt, D_MODEL), row),
            pl.BlockSpec((1, HIST_ROWS, POOL_WIDTH), lambda b, t: (b, 0, 0)),
            pl.BlockSpec((1, D_MODEL), const2),
            pl.BlockSpec((D_MODEL, PROJ_WIDTH), const2),
            pl.BlockSpec((QK_WIDTH, D_MODEL), const2),
            pl.BlockSpec((len(POOL_WINDOWS), POOL_GROUP, POOL_GROUP), lambda b, t: (0, 0, 0)),
            pl.BlockSpec((1, POOL_WIDTH), const2),
        ],
        out_specs=[
            pl.BlockSpec((1, tt, QK_WIDTH), row),
            pl.BlockSpec((1, tt, QK_WIDTH), row),
            pl.BlockSpec((1, tt, QK_WIDTH), row),
            pl.BlockSpec((1, tt, QK_WIDTH), row),
            vb_spec,
            pl.BlockSpec((1, tt, POOL_WIDTH), row),
            pl.BlockSpec((1, POOL_HIST, POOL_WIDTH), lambda b, t: (b, 0, 0)),
        ],
        out_shape=[wide(F32), wide(F32), wide(BF16), wide(BF16), vb_shape, wide(BF16),
                   jax.ShapeDtypeStruct((B, POOL_HIST, POOL_WIDTH), F32)],
        scratch_shapes=[pltpu.VMEM((HIST_ROWS + tt, POOL_WIDTH), F32)],
        compiler_params=pltpu.CompilerParams(
            dimension_semantics=("parallel", "arbitrary"), vmem_limit_bytes=VMEM_LIMIT),
        name="inproj_pool",
    )(x, hist16, g_mix, w_in, w_vt, w_pool, pool_scale)


def _lambda_value(lq1_ref, lk1_ref, lq2_ref, lk2_ref, li):
    s1 = jnp.sum(lq1_ref[...] * lk1_ref[...], axis=-1, keepdims=True)
    s2 = jnp.sum(lq2_ref[...] * lk2_ref[...], axis=-1, keepdims=True)
    return jnp.exp(s1) - jnp.exp(s2) + li


def _split_halves(q):
    lane = lax.broadcasted_iota(jnp.int32, (1, HEAD_WIDTH), 1)
    zero = jnp.zeros_like(q)
    return jnp.where(lane < HEAD_DIM, q, zero), jnp.where(lane >= HEAD_DIM, q, zero)


def _scores(qh, kblk):
    return lax.dot_general(qh, kblk, (((1,), (1,)), ((), ())), preferred_element_type=F32)


def _subln(o, gsub_ref, li):
    return o * lax.rsqrt(jnp.mean(o * o, axis=-1, keepdims=True) + EPS) * gsub_ref[...] * (1.0 - li)


def _attn_prompt_kernel(lq1_ref, lk1_ref, lq2_ref, lk2_ref, gsub_ref, q_ref, k_ref, vt_ref, o_ref,
                        m_ref, l_ref, acc_ref, *, tq, li):
    tk = KV_BLOCK
    qi = pl.program_id(2)
    q_start = qi * tq
    n_full = q_start // tk
    n_sub = tq // Q_SUB
    q2 = []
    for a in range(n_sub):
        q0, q1 = _split_halves(q_ref[0, a * Q_SUB:(a + 1) * Q_SUB, :])
        q2.append(jnp.concatenate([q0, q1], axis=0))
    m_ref[...] = jnp.full(m_ref.shape, NEG_INF, F32)
    l_ref[...] = jnp.zeros(l_ref.shape, F32)
    acc_ref[...] = jnp.zeros(acc_ref.shape, F32)

    def qk(j, masked):
        kblk = k_ref[0, pl.ds(pl.multiple_of(j * tk, tk), tk), :]
        out = []
        for a in range(n_sub):
            s = _scores(kblk, q2[a])
            if masked:
                col = lax.broadcasted_iota(jnp.int32, s.shape, 1)
                ck = (j * tk + lax.broadcasted_iota(jnp.int32, s.shape, 0)) // CHUNK
                cq = (q_start + a * Q_SUB + (col & (Q_SUB - 1))) // CHUNK
                s = jnp.where(ck <= cq, s, NEG_INF)
            out.append((s, jnp.max(s, axis=0, keepdims=True)))
        return tuple(out)

    def consume(j, sc):
        vtb = vt_ref[0, j]
        for a, (s, cmax) in enumerate(sc):
            m_old = m_ref[a]
            m_new = jnp.maximum(m_old, cmax)
            alpha = jnp.exp(m_old - m_new)
            p = jnp.exp(s - m_new)
            l_ref[a] = alpha * l_ref[a] + jnp.sum(p, axis=0, keepdims=True)
            acc_ref[a] = alpha * acc_ref[a] + jnp.dot(vtb, p.astype(BF16), preferred_element_type=F32)
            m_ref[a] = m_new

    def body(j, carry):
        prev, sc = carry
        nxt = qk(j, False)
        consume(prev, sc)
        return j, nxt

    last, sc = lax.fori_loop(0, n_full, body, (n_full, qk(n_full, True)))
    consume(last, sc)
    lam = _lambda_value(lq1_ref, lk1_ref, lq2_ref, lk2_ref, li)
    scale = gsub_ref[...] * (1.0 - li)
    for a in range(n_sub):
        o2 = acc_ref[a] / l_ref[a]
        o_t = o2[:, :Q_SUB] - lam * o2[:, Q_SUB:]
        o_t = o_t * lax.rsqrt(jnp.mean(o_t * o_t, axis=0, keepdims=True) + EPS)
        o_ref[0, a * Q_SUB:(a + 1) * Q_SUB, :] = (o_t.T * scale).astype(BF16)


def _attn_prompt(lams, g_subln, qb, kb, vt, *, tq, li):
    B, T, _ = qb.shape
    assert KV_BLOCK % tq == 0 and tq % Q_SUB == 0
    c2 = lambda b, h, i: (0, 0)
    small = pl.BlockSpec((1, HEAD_DIM), c2)
    return pl.pallas_call(
        functools.partial(_attn_prompt_kernel, tq=tq, li=li),
        grid=(B, N_HEADS, T // tq),
        in_specs=[small, small, small, small,
                  pl.BlockSpec((1, HEAD_WIDTH), c2),
                  pl.BlockSpec((1, tq, HEAD_WIDTH), lambda b, h, i: (b, i, h)),
                  pl.BlockSpec((1, T, HEAD_WIDTH), lambda b, h, i: (b, 0, h)),
                  pl.BlockSpec((1, T // KV_BLOCK, HEAD_WIDTH, KV_BLOCK), lambda b, h, i: (b, 0, h, 0))],
        out_specs=pl.BlockSpec((1, tq, HEAD_WIDTH), lambda b, h, i: (b, i, h)),
        out_shape=jax.ShapeDtypeStruct((B, T, QK_WIDTH), BF16),
        scratch_shapes=[pltpu.VMEM((tq // Q_SUB, 1, 2 * Q_SUB), F32),
                        pltpu.VMEM((tq // Q_SUB, 1, 2 * Q_SUB), F32),
                        pltpu.VMEM((tq // Q_SUB, HEAD_WIDTH, 2 * Q_SUB), F32)],
        compiler_params=pltpu.CompilerParams(
            dimension_semantics=("parallel", "parallel", "arbitrary"), vmem_limit_bytes=VMEM_LIMIT),
        name="attn_prompt",
    )(*lams, g_subln, qb, kb, vt)


def _attn_sample_kernel(lq1_ref, lk1_ref, lq2_ref, lk2_ref, gsub_ref, q_ref, kc_ref, vc_ref,
                        kn_ref, vn_ref, o_ref, *, li):
    q0, q1 = _split_halves(q_ref[0])
    kc = kc_ref[0].astype(BF16)
    vc = vc_ref[0].astype(BF16)
    kn = kn_ref[0]
    vn = vn_ref[0]
    outs = []
    for qh in (q0, q1):
        sc = _scores(qh, kc)
        sn = _scores(qh, kn)
        m = jnp.maximum(jnp.max(sc, axis=-1, keepdims=True), jnp.max(sn, axis=-1, keepdims=True))
        pc = jnp.exp(sc - m)
        pn = jnp.exp(sn - m)
        l = jnp.sum(pc, axis=-1, keepdims=True) + jnp.sum(pn, axis=-1, keepdims=True)
        acc = (jnp.dot(pc.astype(BF16), vc, preferred_element_type=F32)
               + jnp.dot(pn.astype(BF16), vn, preferred_element_type=F32))
        outs.append(acc / l)
    lam = _lambda_value(lq1_ref, lk1_ref, lq2_ref, lk2_ref, li)
    o = outs[0] - lam * outs[1]
    o_ref[0] = _subln(o, gsub_ref, li).astype(BF16)


def _attn_sample(lams, g_subln, qb, cache_k, cache_v, kb, vb, *, li):
    B, T, _ = qb.shape
    P = cache_k.shape[1]
    c2 = lambda b, h: (0, 0)
    small = pl.BlockSpec((1, HEAD_DIM), c2)
    head = lambda rows: pl.BlockSpec((1, rows, HEAD_WIDTH), lambda b, h: (b, 0, h))
    return pl.pallas_call(
        functools.partial(_attn_sample_kernel, li=li),
        grid=(B, N_HEADS),
        in_specs=[small, small, small, small, pl.BlockSpec((1, HEAD_WIDTH), c2),
                  head(T), head(P), head(P), head(T), head(T)],
        out_specs=head(T),
        out_shape=jax.ShapeDtypeStruct((B, T, QK_WIDTH), BF16),
        compiler_params=pltpu.CompilerParams(
            dimension_semantics=("parallel", "parallel"), vmem_limit_bytes=VMEM_LIMIT),
        name="attn_sample",
    )(*lams, g_subln, qb, cache_k, cache_v, kb, vb)


def _lane_first_argmax(vals, lane):
    vmax = jnp.max(vals, axis=-1, keepdims=True)
    idx = jnp.min(jnp.where(vals == vmax, lane, LANES), axis=-1, keepdims=True)
    return vmax, idx


def _outproj_router_kernel(x_ref, pool_ref, attn_ref, wout_ref, gffn_ref, wr_hi_ref, wr_lo_ref, br_ref,
                           x1_ref, h2_ref, comb_ref):
    mix = (jnp.dot(pool_ref[...], wout_ref[0:POOL_WIDTH, :], preferred_element_type=F32)
           + jnp.dot(attn_ref[...], wout_ref[POOL_WIDTH:, :], preferred_element_type=F32))
    x1 = x_ref[...] + mix
    x1_ref[...] = x1
    ms = jnp.mean(x1 * x1, axis=-1, keepdims=True)
    h2 = x1 * lax.rsqrt(ms + EPS) * gffn_ref[...]
    h_hi = h2.astype(BF16)
    h2_ref[...] = h_hi
    h_lo = (h2 - h_hi.astype(F32)).astype(BF16)
    logits = (jnp.dot(h_hi, wr_hi_ref[...], preferred_element_type=F32)
              + jnp.dot(h_hi, wr_lo_ref[...], preferred_element_type=F32)
              + jnp.dot(h_lo, wr_hi_ref[...], preferred_element_type=F32)
              + br_ref[...])
    lane = lax.broadcasted_iota(jnp.int32, logits.shape, 1)
    neg = jnp.float32(-jnp.inf)
    gl = jnp.where(lane < N_GROUPS, logits, neg)
    gmax, g_idx = _lane_first_argmax(gl, lane)
    g_gate = 1.0 / jnp.sum(jnp.exp(gl - gmax), axis=-1, keepdims=True)
    e_lo = ROUTER_LANE0 + EXPERTS_PER_GROUP * g_idx
    el = jnp.where((lane >= e_lo) & (lane < e_lo + EXPERTS_PER_GROUP), logits, neg)
    v1, i1 = _lane_first_argmax(el, lane)
    v2, i2 = _lane_first_argmax(jnp.where(lane == i1, neg, el), lane)
    e21 = jnp.exp(v2 - v1)
    w1 = g_gate / (1.0 + e21)
    w2 = w1 * e21
    comb_ref[...] = jnp.where(lane == i1, w1, 0.0) + jnp.where(lane == i2, w2, 0.0)


def _outproj_router(x2d, pool2d, attn2d, w_out, g_ffn, wr_hi, wr_lo, br, *, tm):
    N = x2d.shape[0]
    row = lambda i: (i, 0)
    c2 = lambda i: (0, 0)
    return pl.pallas_call(
        _outproj_router_kernel,
        grid=(N // tm,),
        in_specs=[pl.BlockSpec((tm, D_MODEL), row),
                  pl.BlockSpec((tm, POOL_WIDTH), row),
                  pl.BlockSpec((tm, QK_WIDTH), row),
                  pl.BlockSpec((D_MODEL, D_MODEL), c2),
                  pl.BlockSpec((1, D_MODEL), c2),
                  pl.BlockSpec((D_MODEL, LANES), c2),
                  pl.BlockSpec((D_MODEL, LANES), c2),
                  pl.BlockSpec((1, LANES), c2)],
        out_specs=[pl.BlockSpec((tm, D_MODEL), row),
                   pl.BlockSpec((tm, D_MODEL), row),
                   pl.BlockSpec((tm, LANES), row)],
        out_shape=[jax.ShapeDtypeStruct((N, D_MODEL), F32),
                   jax.ShapeDtypeStruct((N, D_MODEL), BF16),
                   jax.ShapeDtypeStruct((N, LANES), F32)],
        compiler_params=pltpu.CompilerParams(
            dimension_semantics=("parallel",), vmem_limit_bytes=VMEM_LIMIT),
        name="outproj_router",
    )(x2d, pool2d, attn2d, w_out, g_ffn, wr_hi, wr_lo, br)


def _moe_kernel(h2_ref, comb_ref, x1_ref, wg_ref, wu_ref, wd_ref, gfin_ref, y_ref, acc_ref):
    e = pl.program_id(1)

    @pl.when(e == 0)
    def _():
        acc_ref[...] = jnp.zeros_like(acc_ref)

    comb = comb_ref[...]
    lane = lax.broadcasted_iota(jnp.int32, comb.shape, 1)
    c = jnp.sum(jnp.where(lane == ROUTER_LANE0 + e, comb, 0.0), axis=-1, keepdims=True)
    h = h2_ref[...]
    a = jnp.dot(h, wg_ref[0], preferred_element_type=F32)
    b = jnp.dot(h, wu_ref[0], preferred_element_type=F32)
    he = (a * jax.nn.sigmoid(a) * b).astype(BF16)
    acc_ref[...] += c * jnp.dot(he, wd_ref[0], preferred_element_type=F32)

    @pl.when(e == pl.num_programs(1) - 1)
    def _():
        x2 = x1_ref[...] + acc_ref[...]
        ms = jnp.mean(x2 * x2, axis=-1, keepdims=True)
        y_ref[...] = x2 * lax.rsqrt(ms + EPS) * gfin_ref[...]


def _moe(h2, comb, x1, w_gate, w_up, w_down, g_final, *, tm):
    N = h2.shape[0]
    row = lambda i, e: (i, 0)
    return pl.pallas_call(
        _moe_kernel,
        grid=(N // tm, N_EXPERTS),
        in_specs=[pl.BlockSpec((tm, D_MODEL), row),
                  pl.BlockSpec((tm, LANES), row),
                  pl.BlockSpec((tm, D_MODEL), row),
                  pl.BlockSpec((1, D_MODEL, D_EXPERT), lambda i, e: (e, 0, 0)),
                  pl.BlockSpec((1, D_MODEL, D_EXPERT), lambda i, e: (e, 0, 0)),
                  pl.BlockSpec((1, D_EXPERT, D_MODEL), lambda i, e: (e, 0, 0)),
                  pl.BlockSpec((1, D_MODEL), lambda i, e: (0, 0))],
        out_specs=pl.BlockSpec((tm, D_MODEL), row),
        out_shape=jax.ShapeDtypeStruct((N, D_MODEL), F32),
        scratch_shapes=[pltpu.VMEM((tm, D_MODEL), F32)],
        compiler_params=pltpu.CompilerParams(
            dimension_semantics=("parallel", "arbitrary"), vmem_limit_bytes=VMEM_LIMIT),
        name="moe",
    )(h2, comb, x1, w_gate, w_up, w_down, g_final)


def _layer(x, hist16, cache, start_pos, layer, w, g_final, *, tt, tm):
    B, T, _ = x.shape
    li = _lambda_init(layer)
    k, v, qb, kb, vb, pool, state = _inproj(x, hist16, w["g_mix"], w["w_in"], w["w_vt"], w["w_pool"],
                                            w["pool_scale"], tt=tt, start_pos=start_pos,
                                            transposed_v=cache is None)
    if cache is None:
        attn = _attn_prompt(w["lams"], w["g_subln"], qb, kb, vb, tq=256, li=li)
    else:
        attn = _attn_sample(w["lams"], w["g_subln"], qb, cache[0], cache[1], kb, vb, li=li)
    N = B * T
    x1, h2, comb = _outproj_router(x.reshape(N, D_MODEL), pool.reshape(N, POOL_WIDTH),
                                   attn.reshape(N, QK_WIDTH), w["w_out"], w["g_ffn"],
                                   w["wr_hi"], w["wr_lo"], w["br"], tm=tm)
    y = _moe(h2, comb, x1, w["w_gate"], w["w_up"], w["w_down"], g_final, tm=tm)
    return (y.reshape(B, T, D_MODEL), k.reshape(B, T, N_HEADS, HEAD_WIDTH),
            v.reshape(B, T, N_HEADS, HEAD_WIDTH), state)


def kernel(x_prompt, x_sample, cache_k, cache_v, state_pool, g_mix, w_in, w_pool, pool_scale, lam_q1, lam_k1, lam_q2, lam_k2, g_subln, w_out, g_ffn, w_group, b_group, w_erouter, b_erouter, w_gate, w_up, w_down, g_final):
    depth = g_mix.shape[0]
    assert depth == 1, "the final norm is fused into the layer, so exactly one layer is supported"
    l = 0
    wr = jnp.concatenate([w_group[l], w_erouter[l]], axis=1)
    wr = jnp.pad(wr, ((0, 0), (0, LANES - wr.shape[1])))
    wr_hi = wr.astype(BF16)
    wr_lo = (wr - wr_hi.astype(F32)).astype(BF16)
    br = jnp.pad(jnp.concatenate([b_group[l], b_erouter[l]]), (0, LANES - N_GROUPS - N_EXPERTS))
    w = dict(
        g_mix=g_mix[l][None], w_in=w_in[l].astype(BF16),
        w_vt=w_in[l][:, POOL_WIDTH + 2 * QK_WIDTH:].T.astype(BF16), w_pool=w_pool[l].astype(BF16),
        pool_scale=pool_scale[l][None],
        lams=(lam_q1[l][None], lam_k1[l][None], lam_q2[l][None], lam_k2[l][None]),
        g_subln=g_subln[l][None], w_out=w_out[l].astype(BF16), g_ffn=g_ffn[l][None],
        wr_hi=wr_hi, wr_lo=wr_lo, br=br[None],
        w_gate=w_gate[l].astype(BF16), w_up=w_up[l].astype(BF16), w_down=w_down[l].astype(BF16),
    )
    gfin = g_final[None]
    Bp = x_prompt.shape[0]
    Bs, _, _ = x_sample.shape
    P = cache_k.shape[2]
    zero_hist = jnp.zeros((Bp, HIST_ROWS, POOL_WIDTH), F32)
    samp_hist = jnp.pad(state_pool[l], ((0, 0), (HIST_ROWS - POOL_HIST, 0), (0, 0)))
    cache = (cache_k[l].reshape(Bs, P, QK_WIDTH), cache_v[l].reshape(Bs, P, QK_WIDTH))

    yp, kp, vp, pp = _layer(x_prompt, zero_hist, None, 0, l, w, gfin, tt=512, tm=512)
    ys, kn, vn, pn = _layer(x_sample, samp_hist, cache, P, l, w, gfin, tt=64, tm=512)
    return (yp, ys, kp[None], vp[None], pp[None], kn[None], vn[None], pn[None])
```

```python
import functools
import math

import jax
import jax.numpy as jnp
from jax import lax
from jax.experimental import pallas as pl
from jax.experimental.pallas import tpu as pltpu

F32 = jnp.float32
BF16 = jnp.bfloat16

D_MODEL = 1024
CHUNK = 64
POOL_WIDTH = 512
POOL_WINDOWS = (2, 4, 8, 16)
POOL_GROUP = 128
POOL_HIST = 15
HIST_ROWS = 16
N_HEADS = 4
HEAD_DIM = 64
HEAD_WIDTH = 128
QK_WIDTH = 512
PROJ_WIDTH = 2048
ATTN_SCALE = HEAD_DIM ** -0.5
NEG_INF = -1e30
N_GROUPS = 4
EXPERTS_PER_GROUP = 4
N_EXPERTS = 16
D_EXPERT = 256
EPS = 1e-6
LANES = 128
KV_BLOCK = 256
Q_SUB = 128
ROUTER_LANE0 = N_GROUPS
ROUTE_TILE = 256
SEG_ALIGN = 16
ROW_W = D_MODEL + LANES
MOE_BLOCK = 512
ZERO_ROWS = 256
ZERO_FILL_ROWS = 2 * MOE_BLOCK
VMEM_LIMIT = 48 * 1024 * 1024


def _lambda_init(layer):
    return 0.8 - 0.6 * math.exp(-0.3 * layer)


def _inproj_kernel(x_ref, hist_ref, gmix_ref, win_ref, wvt_ref, wpool_ref, pscale_ref,
                   k_ref, v_ref, qb_ref, kb_ref, vb_ref, pool_ref, state_ref,
                   ext_ref, *, tt, start_pos, transposed_v):
    t = pl.program_id(1)
    x = x_ref[0]
    ms = jnp.mean(x * x, axis=-1, keepdims=True)
    h = (x * lax.rsqrt(ms + EPS) * gmix_ref[...]).astype(BF16)
    proj = jnp.dot(h, win_ref[...], preferred_element_type=F32)
    u = proj[:, :POOL_WIDTH]
    q = proj[:, POOL_WIDTH:POOL_WIDTH + QK_WIDTH]
    k = proj[:, POOL_WIDTH + QK_WIDTH:POOL_WIDTH + 2 * QK_WIDTH]
    v = proj[:, POOL_WIDTH + 2 * QK_WIDTH:]
    k_ref[0] = k
    v_ref[0] = v
    qb_ref[0] = (q * ATTN_SCALE).astype(BF16)
    kb_ref[0] = k.astype(BF16)
    if transposed_v:
        vt = lax.dot_general(wvt_ref[...], h, (((1,), (1,)), ((), ())),
                             preferred_element_type=F32).astype(BF16)
        for i in range(tt // KV_BLOCK):
            vb_ref[0, i] = vt[:, i * KV_BLOCK:(i + 1) * KV_BLOCK]
    else:
        vb_ref[0] = v.astype(BF16)

    @pl.when(t == 0)
    def _():
        ext_ref[0:HIST_ROWS] = hist_ref[0]

    @pl.when(t > 0)
    def _():
        ext_ref[0:HIST_ROWS] = ext_ref[tt:tt + HIST_ROWS]

    ext_ref[HIST_ROWS:HIST_ROWS + tt] = u

    pos = start_pos + t * tt + lax.broadcasted_iota(jnp.int32, (tt, 1), 0)
    outs = []
    for g, w in enumerate(POOL_WINDOWS):
        lo = g * POOL_GROUP
        ug = u[:, lo:lo + POOL_GROUP]
        acc = ug
        for j in range(1, w):
            acc = acc + ext_ref[HIST_ROWS - j:HIST_ROWS - j + tt, lo:lo + POOL_GROUP]
        inv_cnt = 1.0 / jnp.minimum(w, pos + 1).astype(F32)
        d = acc * inv_cnt - ug
        outs.append(jnp.dot(d.astype(BF16), wpool_ref[g], preferred_element_type=F32))
    y = jnp.concatenate(outs, axis=-1) * pscale_ref[...]
    pool_ref[0] = y.astype(BF16)

    @pl.when(t == pl.num_programs(1) - 1)
    def _():
        state_ref[0] = ext_ref[tt + 1:tt + HIST_ROWS]


def _inproj(x, hist16, g_mix, w_in, w_vt, w_pool, pool_scale, *, tt, start_pos, transposed_v):
    B, T, _ = x.shape
    nt = T // tt
    row = lambda b, t: (b, t, 0)
    const2 = lambda b, t: (0, 0)
    wide = lambda dt: jax.ShapeDtypeStruct((B, T, QK_WIDTH), dt)
    if transposed_v:
        vb_spec = pl.BlockSpec((1, tt // KV_BLOCK, QK_WIDTH, KV_BLOCK), lambda b, t: (b, t, 0, 0))
        vb_shape = jax.ShapeDtypeStruct((B, T // KV_BLOCK, QK_WIDTH, KV_BLOCK), BF16)
    else:
        vb_spec = pl.BlockSpec((1, tt, QK_WIDTH), row)
        vb_shape = wide(BF16)
    return pl.pallas_call(
        functools.partial(_inproj_kernel, tt=tt, start_pos=start_pos, transposed_v=transposed_v),
        grid=(B, nt),
        in_specs=[
            pl.BlockSpec((1, tt, D_MODEL), row),
            pl.BlockSpec((1, HIST_ROWS, POOL_WIDTH), lambda b, t: (b, 0, 0)),
            pl.BlockSpec((1, D_MODEL), const2),
            pl.BlockSpec((D_MODEL, PROJ_WIDTH), const2),
            pl.BlockSpec((QK_WIDTH, D_MODEL), const2),
            pl.BlockSpec((len(POOL_WINDOWS), POOL_GROUP, POOL_GROUP), lambda b, t: (0, 0, 0)),
            pl.BlockSpec((1, POOL_WIDTH), const2),
        ],
        out_specs=[
            pl.BlockSpec((1, tt, QK_WIDTH), row),
            pl.BlockSpec((1, tt, QK_WIDTH), row),
            pl.BlockSpec((1, tt, QK_WIDTH), row),
            pl.BlockSpec((1, tt, QK_WIDTH), row),
            vb_spec,
            pl.BlockSpec((1, tt, POOL_WIDTH), row),
            pl.BlockSpec((1, POOL_HIST, POOL_WIDTH), lambda b, t: (b, 0, 0)),
        ],
        out_shape=[wide(F32), wide(F32), wide(BF16), wide(BF16), vb_shape, wide(BF16),
                   jax.ShapeDtypeStruct((B, POOL_HIST, POOL_WIDTH), F32)],
        scratch_shapes=[pltpu.VMEM((HIST_ROWS + tt, POOL_WIDTH), F32)],
        compiler_params=pltpu.CompilerParams(
            dimension_semantics=("parallel", "arbitrary"), vmem_limit_bytes=VMEM_LIMIT),
        name="inproj_pool",
    )(x, hist16, g_mix, w_in, w_vt, w_pool, pool_scale)


def _lambda_value(lq1_ref, lk1_ref, lq2_ref, lk2_ref, li):
    s1 = jnp.sum(lq1_ref[...] * lk1_ref[...], axis=-1, keepdims=True)
    s2 = jnp.sum(lq2_ref[...] * lk2_ref[...], axis=-1, keepdims=True)
    return jnp.exp(s1) - jnp.exp(s2) + li


def _split_halves(q):
    lane = lax.broadcasted_iota(jnp.int32, (1, HEAD_WIDTH), 1)
    zero = jnp.zeros_like(q)
    return jnp.where(lane < HEAD_DIM, q, zero), jnp.where(lane >= HEAD_DIM, q, zero)


def _scores(qh, kblk):
    return lax.dot_general(qh, kblk, (((1,), (1,)), ((), ())), preferred_element_type=F32)


def _subln(o, gsub_ref, li):
    return o * lax.rsqrt(jnp.mean(o * o, axis=-1, keepdims=True) + EPS) * gsub_ref[...] * (1.0 - li)


def _attn_prompt_kernel(lq1_ref, lk1_ref, lq2_ref, lk2_ref, gsub_ref, q_ref, k_ref, vt_ref, o_ref,
                        m_ref, l_ref, acc_ref, *, tq, li):
    tk = KV_BLOCK
    qi = pl.program_id(2)
    q_start = qi * tq
    n_full = q_start // tk
    n_sub = tq // Q_SUB
    q2 = []
    for a in range(n_sub):
        q0, q1 = _split_halves(q_ref[0, a * Q_SUB:(a + 1) * Q_SUB, :])
        q2.append(jnp.concatenate([q0, q1], axis=0))
    m_ref[...] = jnp.full(m_ref.shape, NEG_INF, F32)
    l_ref[...] = jnp.zeros(l_ref.shape, F32)
    acc_ref[...] = jnp.zeros(acc_ref.shape, F32)

    def qk(j, masked):
        kblk = k_ref[0, pl.ds(pl.multiple_of(j * tk, tk), tk), :]
        out = []
        for a in range(n_sub):
            s = _scores(kblk, q2[a])
            if masked:
                col = lax.broadcasted_iota(jnp.int32, s.shape, 1)
                ck = (j * tk + lax.broadcasted_iota(jnp.int32, s.shape, 0)) // CHUNK
                cq = (q_start + a * Q_SUB + (col & (Q_SUB - 1))) // CHUNK
                s = jnp.where(ck <= cq, s, NEG_INF)
            out.append((s, jnp.max(s, axis=0, keepdims=True)))
        return tuple(out)

    def consume(j, sc):
        vtb = vt_ref[0, j]
        for a, (s, cmax) in enumerate(sc):
            m_old = m_ref[a]
            m_new = jnp.maximum(m_old, cmax)
            alpha = jnp.exp(m_old - m_new)
            p = jnp.exp(s - m_new)
            l_ref[a] = alpha * l_ref[a] + jnp.sum(p, axis=0, keepdims=True)
            acc_ref[a] = alpha * acc_ref[a] + jnp.dot(vtb, p.astype(BF16), preferred_element_type=F32)
            m_ref[a] = m_new

    def body(j, carry):
        prev, sc = carry
        nxt = qk(j, False)
        consume(prev, sc)
        return j, nxt

    last, sc = lax.fori_loop(0, n_full, body, (n_full, qk(n_full, True)))
    consume(last, sc)
    lam = _lambda_value(lq1_ref, lk1_ref, lq2_ref, lk2_ref, li)
    scale = gsub_ref[...] * (1.0 - li)
    for a in range(n_sub):
        o2 = acc_ref[a] / l_ref[a]
        o_t = o2[:, :Q_SUB] - lam * o2[:, Q_SUB:]
        o_t = o_t * lax.rsqrt(jnp.mean(o_t * o_t, axis=0, keepdims=True) + EPS)
        o_ref[0, a * Q_SUB:(a + 1) * Q_SUB, :] = (o_t.T * scale).astype(BF16)


def _attn_prompt(lams, g_subln, qb, kb, vt, *, tq, li):
    B, T, _ = qb.shape
    assert KV_BLOCK % tq == 0 and tq % Q_SUB == 0
    c2 = lambda b, h, i: (0, 0)
    small = pl.BlockSpec((1, HEAD_DIM), c2)
    return pl.pallas_call(
        functools.partial(_attn_prompt_kernel, tq=tq, li=li),
        grid=(B, N_HEADS, T // tq),
        in_specs=[small, small, small, small,
                  pl.BlockSpec((1, HEAD_WIDTH), c2),
                  pl.BlockSpec((1, tq, HEAD_WIDTH), lambda b, h, i: (b, i, h)),
                  pl.BlockSpec((1, T, HEAD_WIDTH), lambda b, h, i: (b, 0, h)),
                  pl.BlockSpec((1, T // KV_BLOCK, HEAD_WIDTH, KV_BLOCK), lambda b, h, i: (b, 0, h, 0))],
        out_specs=pl.BlockSpec((1, tq, HEAD_WIDTH), lambda b, h, i: (b, i, h)),
        out_shape=jax.ShapeDtypeStruct((B, T, QK_WIDTH), BF16),
        scratch_shapes=[pltpu.VMEM((tq // Q_SUB, 1, 2 * Q_SUB), F32),
                        pltpu.VMEM((tq // Q_SUB, 1, 2 * Q_SUB), F32),
                        pltpu.VMEM((tq // Q_SUB, HEAD_WIDTH, 2 * Q_SUB), F32)],
        compiler_params=pltpu.CompilerParams(
            dimension_semantics=("parallel", "parallel", "arbitrary"), vmem_limit_bytes=VMEM_LIMIT),
        name="attn_prompt",
    )(*lams, g_subln, qb, kb, vt)


def _attn_sample_kernel(lq1_ref, lk1_ref, lq2_ref, lk2_ref, gsub_ref, q_ref, kc_ref, vc_ref,
                        kn_ref, vn_ref, o_ref, *, li):
    q0, q1 = _split_halves(q_ref[0])
    kc = kc_ref[0].astype(BF16)
    vc = vc_ref[0].astype(BF16)
    kn = kn_ref[0]
    vn = vn_ref[0]
    outs = []
    for qh in (q0, q1):
        sc = _scores(qh, kc)
        sn = _scores(qh, kn)
        m = jnp.maximum(jnp.max(sc, axis=-1, keepdims=True), jnp.max(sn, axis=-1, keepdims=True))
        pc = jnp.exp(sc - m)
        pn = jnp.exp(sn - m)
        l = jnp.sum(pc, axis=-1, keepdims=True) + jnp.sum(pn, axis=-1, keepdims=True)
        acc = (jnp.dot(pc.astype(BF16), vc, preferred_element_type=F32)
               + jnp.dot(pn.astype(BF16), vn, preferred_element_type=F32))
        outs.append(acc / l)
    lam = _lambda_value(lq1_ref, lk1_ref, lq2_ref, lk2_ref, li)
    o = outs[0] - lam * outs[1]
    o_ref[0] = _subln(o, gsub_ref, li).astype(BF16)


def _attn_sample(lams, g_subln, qb, cache_k, cache_v, kb, vb, *, li):
    B, T, _ = qb.shape
    P = cache_k.shape[1]
    c2 = lambda b, h: (0, 0)
    small = pl.BlockSpec((1, HEAD_DIM), c2)
    head = lambda rows: pl.BlockSpec((1, rows, HEAD_WIDTH), lambda b, h: (b, 0, h))
    return pl.pallas_call(
        functools.partial(_attn_sample_kernel, li=li),
        grid=(B, N_HEADS),
        in_specs=[small, small, small, small, pl.BlockSpec((1, HEAD_WIDTH), c2),
                  head(T), head(P), head(P), head(T), head(T)],
        out_specs=head(T),
        out_shape=jax.ShapeDtypeStruct((B, T, QK_WIDTH), BF16),
        compiler_params=pltpu.CompilerParams(
            dimension_semantics=("parallel", "parallel"), vmem_limit_bytes=VMEM_LIMIT),
        name="attn_sample",
    )(*lams, g_subln, qb, cache_k, cache_v, kb, vb)


def _lane_first_argmax(vals, lane):
    vmax = jnp.max(vals, axis=-1, keepdims=True)
    idx = jnp.min(jnp.where(vals == vmax, lane, LANES), axis=-1, keepdims=True)
    return vmax, idx


def _bf16_split(x):
    hi = x.astype(BF16)
    return hi, (x - hi.astype(F32)).astype(BF16)


def _route_sort_kernel(x_ref, pool_ref, attn_ref, wout_ref, gffn_ref, wr_hi_ref, wr_lo_ref, br_ref,
                       x1_ref, route_ref, seg_ref, hs_ref,
                       stage_ref, zero_ref, run_ref, sem_ref, *, tm, cap, ntiles):
    i = pl.program_id(0)
    slot = lax.rem(i, 2)
    slots = tm + N_GROUPS * SEG_ALIGN

    @pl.when(i == 0)
    def _():
        for g in range(N_GROUPS):
            run_ref[g] = 0
        stage_ref[...] = jnp.zeros(stage_ref.shape, BF16)
        zero_ref[...] = jnp.zeros(zero_ref.shape, BF16)

    mix = (jnp.dot(pool_ref[...], wout_ref[0:POOL_WIDTH, :], preferred_element_type=F32)
           + jnp.dot(attn_ref[...], wout_ref[POOL_WIDTH:, :], preferred_element_type=F32))
    x1 = x_ref[...] + mix
    x1_ref[...] = x1
    ms = jnp.mean(x1 * x1, axis=-1, keepdims=True)
    h2 = x1 * lax.rsqrt(ms + EPS) * gffn_ref[...]
    h_hi, h_lo = _bf16_split(h2)
    logits = (jnp.dot(h_hi, wr_hi_ref[...], preferred_element_type=F32)
              + jnp.dot(h_hi, wr_lo_ref[...], preferred_element_type=F32)
              + jnp.dot(h_lo, wr_hi_ref[...], preferred_element_type=F32)
              + br_ref[...])
    lane = lax.broadcasted_iota(jnp.int32, logits.shape, 1)
    neg = jnp.float32(-jnp.inf)
    gl = jnp.where(lane < N_GROUPS, logits, neg)
    gmax, g_idx = _lane_first_argmax(gl, lane)
    g_gate = 1.0 / jnp.sum(jnp.exp(gl - gmax), axis=-1, keepdims=True)
    e_lo = ROUTER_LANE0 + EXPERTS_PER_GROUP * g_idx
    el = jnp.where((lane >= e_lo) & (lane < e_lo + EXPERTS_PER_GROUP), logits, neg)
    v1, i1 = _lane_first_argmax(el, lane)
    v2, i2 = _lane_first_argmax(jnp.where(lane == i1, neg, el), lane)
    e21 = jnp.exp(v2 - v1)
    w1 = g_gate / (1.0 + e21)
    w2 = w1 * e21
    k1 = i1 - e_lo
    k2 = i2 - e_lo
    w1h = w1.astype(BF16).astype(F32)
    w2h = w2.astype(BF16).astype(F32)
    cparts = (jnp.where(lane == k1, w1h, 0.0) + jnp.where(lane == k2, w2h, 0.0)
              + jnp.where(lane == k1 + EXPERTS_PER_GROUP, w1 - w1h, 0.0)
              + jnp.where(lane == k2 + EXPERTS_PER_GROUP, w2 - w2h, 0.0))

    onehot = jnp.where(lane == g_idx, 1.0, 0.0)
    before = (lax.broadcasted_iota(jnp.int32, (tm, tm), 1)
              < lax.broadcasted_iota(jnp.int32, (tm, tm), 0))
    rank_all = jnp.dot(jnp.where(before, 1.0, 0.0).astype(BF16), onehot.astype(BF16),
                       preferred_element_type=F32)
    n = jnp.sum(onehot, axis=0, keepdims=True)
    n_pad = jnp.floor((n + (SEG_ALIGN - 1)) * (1.0 / SEG_ALIGN)) * SEG_ALIGN
    upper = (lax.broadcasted_iota(jnp.int32, (LANES, LANES), 0)
             < lax.broadcasted_iota(jnp.int32, (LANES, LANES), 1))
    off = jnp.dot(jnp.broadcast_to(n_pad, (8, LANES)).astype(BF16), jnp.where(upper, 1.0, 0.0).astype(BF16),
                  preferred_element_type=F32)[0:1]
    rank = jnp.sum(onehot * rank_all, axis=-1, keepdims=True)
    local = jnp.sum(onehot * off, axis=-1, keepdims=True) + rank
    route_ref[...] = jnp.broadcast_to(g_idx.astype(F32) * tm + rank, (tm, LANES))

    place = jnp.where(lax.broadcasted_iota(jnp.int32, (tm, slots), 1).astype(F32) == local, 1.0, 0.0)
    rows = jnp.concatenate([h_hi, cparts.astype(BF16)], axis=1)
    sorted_rows = lax.dot_general(place.astype(BF16), rows, (((0,), (0,)), ((), ())),
                                  preferred_element_type=F32)
    stage_ref[slot, 0:slots, :] = sorted_rows.astype(BF16)

    def window(g, slot_, src_row, dst_row):
        return pltpu.make_async_copy(
            stage_ref.at[slot_, pl.ds(pl.multiple_of(src_row, SEG_ALIGN), tm)],
            hs_ref.at[pl.ds(pl.multiple_of(dst_row, SEG_ALIGN), tm)],
            sem_ref.at[slot_])

    @pl.when(i > 0)
    def _():
        for g in range(N_GROUPS):
            window(g, 1 - slot, 0, 0).wait()

    for g in range(N_GROUPS):
        src = off[0, g].astype(jnp.int32)
        cnt = n_pad[0, g].astype(jnp.int32)
        start = run_ref[g]
        seg_ref[g, i] = start
        window(g, slot, src, g * cap + start).start()
        run_ref[g] = start + cnt

    @pl.when(i == ntiles - 1)
    def _():
        for g in range(N_GROUPS):
            window(g, slot, 0, 0).wait()
        fills = []
        for g in range(N_GROUPS):
            total = run_ref[g]
            seg_ref[g, ntiles] = total
            for z in range(ZERO_FILL_ROWS // ZERO_ROWS):
                dst = g * cap + total + z * ZERO_ROWS
                fills.append(pltpu.make_async_copy(
                    zero_ref, hs_ref.at[pl.ds(pl.multiple_of(dst, SEG_ALIGN), ZERO_ROWS)],
                    sem_ref.at[2]))
        for f in fills:
            f.start()
        for f in fills:
            f.wait()


def _round_up(x, m):
    return (x + m - 1) // m * m


def _sorted_capacity(n_tokens, ntiles):
    return _round_up(n_tokens + SEG_ALIGN * ntiles, MOE_BLOCK) + 3 * MOE_BLOCK


def _route_sort(x2d, pool2d, attn2d, w_out, g_ffn, wr_hi, wr_lo, br, *, tm):
    N = x2d.shape[0]
    ntiles = N // tm
    cap = _sorted_capacity(N, ntiles)
    row = lambda i: (i, 0)
    c2 = lambda i: (0, 0)
    return pl.pallas_call(
        functools.partial(_route_sort_kernel, tm=tm, cap=cap, ntiles=ntiles),
        grid=(ntiles,),
        in_specs=[pl.BlockSpec((tm, D_MODEL), row),
                  pl.BlockSpec((tm, POOL_WIDTH), row),
                  pl.BlockSpec((tm, QK_WIDTH), row),
                  pl.BlockSpec((D_MODEL, D_MODEL), c2),
                  pl.BlockSpec((1, D_MODEL), c2),
                  pl.BlockSpec((D_MODEL, LANES), c2),
                  pl.BlockSpec((D_MODEL, LANES), c2),
                  pl.BlockSpec((1, LANES), c2)],
        out_specs=[pl.BlockSpec((tm, D_MODEL), row),
                   pl.BlockSpec((tm, LANES), row),
                   pl.BlockSpec(memory_space=pltpu.SMEM),
                   pl.BlockSpec(memory_space=pl.ANY)],
        out_shape=[jax.ShapeDtypeStruct((N, D_MODEL), F32),
                   jax.ShapeDtypeStruct((N, LANES), F32),
                   jax.ShapeDtypeStruct((N_GROUPS, ntiles + 1), jnp.int32),
                   jax.ShapeDtypeStruct((N_GROUPS * cap, ROW_W), BF16)],
        scratch_shapes=[pltpu.VMEM((2, 2 * tm + N_GROUPS * SEG_ALIGN, ROW_W), BF16),
                        pltpu.VMEM((ZERO_ROWS, ROW_W), BF16),
                        pltpu.SMEM((N_GROUPS,), jnp.int32),
                        pltpu.SemaphoreType.DMA((3,))],
        compiler_params=pltpu.CompilerParams(
            dimension_semantics=("arbitrary",), vmem_limit_bytes=VMEM_LIMIT),
        name="route_sort",
    )(x2d, pool2d, attn2d, w_out, g_ffn, wr_hi, wr_lo, br)


def _group_moe_kernel(blk_ref, grp_ref, valid_ref, hs_ref, wg_ref, wu_ref, wd_ref, out_ref):
    b = pl.program_id(0)

    @pl.when(valid_ref[b] == 1)
    def _():
        rows = hs_ref[...]
        x = rows[:, :D_MODEL]
        cparts = rows[:, D_MODEL:].astype(F32)
        lane = lax.broadcasted_iota(jnp.int32, cparts.shape, 1)
        hes = []
        for e in range(EXPERTS_PER_GROUP):
            a = jnp.dot(x, wg_ref[e], preferred_element_type=F32)
            u = jnp.dot(x, wu_ref[e], preferred_element_type=F32)
            c = jnp.sum(jnp.where((lane == e) | (lane == e + EXPERTS_PER_GROUP), cparts, 0.0),
                        axis=-1, keepdims=True)
            hes.append((a * jax.nn.sigmoid(a) * u * c).astype(BF16))
        he = jnp.concatenate(hes, axis=1)
        out_ref[...] = jnp.dot(he, wd_ref[0], preferred_element_type=F32).astype(BF16)

    @pl.when(valid_ref[b] == 0)
    def _():
        out_ref[...] = jnp.zeros(out_ref.shape, BF16)


def _group_moe(blk, grp, valid, hs, w_gate, w_up, w_down_grouped):
    nb = blk.shape[0]
    bm = MOE_BLOCK
    by_block = lambda b, blk, grp, valid: (blk[b], 0)
    by_group = lambda b, blk, grp, valid: (grp[b], 0, 0)
    grid_spec = pltpu.PrefetchScalarGridSpec(
        num_scalar_prefetch=3,
        grid=(nb,),
        in_specs=[pl.BlockSpec((bm, ROW_W), by_block),
                  pl.BlockSpec((EXPERTS_PER_GROUP, D_MODEL, D_EXPERT), by_group),
                  pl.BlockSpec((EXPERTS_PER_GROUP, D_MODEL, D_EXPERT), by_group),
                  pl.BlockSpec((1, EXPERTS_PER_GROUP * D_EXPERT, D_MODEL), by_group)],
        out_specs=pl.BlockSpec((bm, D_MODEL), by_block),
    )
    return pl.pallas_call(
        _group_moe_kernel,
        grid_spec=grid_spec,
        out_shape=jax.ShapeDtypeStruct((hs.shape[0], D_MODEL), BF16),
        compiler_params=pltpu.CompilerParams(
            dimension_semantics=("arbitrary",), vmem_limit_bytes=VMEM_LIMIT),
        name="group_moe",
    )(blk, grp, valid, hs, w_gate, w_up, w_down_grouped)


def _block_table(seg, n_tokens, ntiles, cap):
    bm = MOE_BLOCK
    nb_max = _round_up(n_tokens + SEG_ALIGN * ntiles, bm) // bm + 2 * N_GROUPS
    total = seg[:, ntiles]
    nb = (total + bm - 1) // bm + 1
    end = jnp.cumsum(nb)
    start = end - nb
    b = jnp.arange(nb_max, dtype=jnp.int32)
    g = jnp.sum((b[:, None] >= end[None, :]).astype(jnp.int32), axis=1)
    valid = g < N_GROUPS
    gc = jnp.minimum(g, N_GROUPS - 1)
    blk = jnp.where(valid, gc * (cap // bm) + b - start[gc], N_GROUPS * cap // bm - 1)
    return blk.astype(jnp.int32), gc.astype(jnp.int32), valid.astype(jnp.int32)


def _unsort_norm_kernel(seg_ref, x1_ref, route_ref, gfin_ref, os_ref, y_ref, win_ref, sem_ref,
                        *, tm, cap, ntiles):
    i = pl.program_id(0)
    slot = lax.rem(i, 2)

    def window(tile, g, slot_):
        start = seg_ref[g * (ntiles + 1) + tile]
        return pltpu.make_async_copy(
            os_ref.at[pl.ds(pl.multiple_of(g * cap + start, SEG_ALIGN), tm)],
            win_ref.at[slot_, g], sem_ref.at[slot_])

    @pl.when(i == 0)
    def _():
        for g in range(N_GROUPS):
            window(0, g, 0).start()

    @pl.when(i + 1 < ntiles)
    def _():
        for g in range(N_GROUPS):
            window(i + 1, g, 1 - slot).start()

    for g in range(N_GROUPS):
        window(i, g, slot).wait()

    where = route_ref[:, 0:1]
    pick = jnp.where(lax.broadcasted_iota(jnp.int32, (tm, N_GROUPS * tm), 1).astype(F32) == where, 1.0, 0.0)
    moe = jnp.dot(pick.astype(BF16), win_ref[slot].reshape(N_GROUPS * tm, D_MODEL),
                  preferred_element_type=F32)
    x2 = x1_ref[...] + moe
    ms = jnp.mean(x2 * x2, axis=-1, keepdims=True)
    y_ref[...] = x2 * lax.rsqrt(ms + EPS) * gfin_ref[...]


def _unsort_norm(seg, x1, route, g_final, out_sorted, *, tm, cap):
    N = x1.shape[0]
    ntiles = N // tm
    grid_spec = pltpu.PrefetchScalarGridSpec(
        num_scalar_prefetch=1,
        grid=(ntiles,),
        in_specs=[pl.BlockSpec((tm, D_MODEL), lambda i, seg: (i, 0)),
                  pl.BlockSpec((tm, LANES), lambda i, seg: (i, 0)),
                  pl.BlockSpec((1, D_MODEL), lambda i, seg: (0, 0)),
                  pl.BlockSpec(memory_space=pl.ANY)],
        out_specs=pl.BlockSpec((tm, D_MODEL), lambda i, seg: (i, 0)),
        scratch_shapes=[pltpu.VMEM((2, N_GROUPS, tm, D_MODEL), BF16),
                        pltpu.SemaphoreType.DMA((2,))],
    )
    return pl.pallas_call(
        functools.partial(_unsort_norm_kernel, tm=tm, cap=cap, ntiles=ntiles),
        grid_spec=grid_spec,
        out_shape=jax.ShapeDtypeStruct((N, D_MODEL), F32),
        compiler_params=pltpu.CompilerParams(
            dimension_semantics=("arbitrary",), vmem_limit_bytes=VMEM_LIMIT),
        name="unsort_norm",
    )(seg.reshape(-1), x1, route, g_final, out_sorted)


def _layer(x, hist16, cache, start_pos, layer, w, g_final, *, tt, tm):
    B, T, _ = x.shape
    li = _lambda_init(layer)
    k, v, qb, kb, vb, pool, state = _inproj(x, hist16, w["g_mix"], w["w_in"], w["w_vt"], w["w_pool"],
                                            w["pool_scale"], tt=tt, start_pos=start_pos,
                                            transposed_v=cache is None)
    if cache is None:
        attn = _attn_prompt(w["lams"], w["g_subln"], qb, kb, vb, tq=256, li=li)
    else:
        attn = _attn_sample(w["lams"], w["g_subln"], qb, cache[0], cache[1], kb, vb, li=li)
    N = B * T
    ntiles = N // tm
    cap = _sorted_capacity(N, ntiles)
    x1, route, seg, hs = _route_sort(x.reshape(N, D_MODEL), pool.reshape(N, POOL_WIDTH),
                                     attn.reshape(N, QK_WIDTH), w["w_out"], w["g_ffn"],
                                     w["wr_hi"], w["wr_lo"], w["br"], tm=tm)
    blk, grp, valid = _block_table(seg, N, ntiles, cap)
    out_sorted = _group_moe(blk, grp, valid, hs, w["w_gate"], w["w_up"], w["w_down"])
    y = _unsort_norm(seg, x1, route, g_final, out_sorted, tm=tm, cap=cap)
    return (y.reshape(B, T, D_MODEL), k.reshape(B, T, N_HEADS, HEAD_WIDTH),
            v.reshape(B, T, N_HEADS, HEAD_WIDTH), state)


def kernel(x_prompt, x_sample, cache_k, cache_v, state_pool, g_mix, w_in, w_pool, pool_scale, lam_q1, lam_k1, lam_q2, lam_k2, g_subln, w_out, g_ffn, w_group, b_group, w_erouter, b_erouter, w_gate, w_up, w_down, g_final):
    depth = g_mix.shape[0]
    assert depth == 1, "the final norm is fused into the layer, so exactly one layer is supported"
    l = 0
    wr = jnp.concatenate([w_group[l], w_erouter[l]], axis=1)
    wr = jnp.pad(wr, ((0, 0), (0, LANES - wr.shape[1])))
    wr_hi = wr.astype(BF16)
    wr_lo = (wr - wr_hi.astype(F32)).astype(BF16)
    br = jnp.pad(jnp.concatenate([b_group[l], b_erouter[l]]), (0, LANES - N_GROUPS - N_EXPERTS))
    w = dict(
        g_mix=g_mix[l][None], w_in=w_in[l].astype(BF16),
        w_vt=w_in[l][:, POOL_WIDTH + 2 * QK_WIDTH:].T.astype(BF16), w_pool=w_pool[l].astype(BF16),
        pool_scale=pool_scale[l][None],
        lams=(lam_q1[l][None], lam_k1[l][None], lam_q2[l][None], lam_k2[l][None]),
        g_subln=g_subln[l][None], w_out=w_out[l].astype(BF16), g_ffn=g_ffn[l][None],
        wr_hi=wr_hi, wr_lo=wr_lo, br=br[None],
        w_gate=w_gate[l].astype(BF16), w_up=w_up[l].astype(BF16),
        w_down=w_down[l].astype(BF16).reshape(N_GROUPS, EXPERTS_PER_GROUP * D_EXPERT, D_MODEL),
    )
    gfin = g_final[None]
    Bp = x_prompt.shape[0]
    Bs, _, _ = x_sample.shape
    P = cache_k.shape[2]
    zero_hist = jnp.zeros((Bp, HIST_ROWS, POOL_WIDTH), F32)
    samp_hist = jnp.pad(state_pool[l], ((0, 0), (HIST_ROWS - POOL_HIST, 0), (0, 0)))
    cache = (cache_k[l].reshape(Bs, P, QK_WIDTH), cache_v[l].reshape(Bs, P, QK_WIDTH))

    yp, kp, vp, pp = _layer(x_prompt, zero_hist, None, 0, l, w, gfin, tt=512, tm=ROUTE_TILE)
    ys, kn, vn, pn = _layer(x_sample, samp_hist, cache, P, l, w, gfin, tt=64, tm=ROUTE_TILE)
    return (yp, ys, kp[None], vp[None], pp[None], kn[None], vn[None], pn[None])
```

```python
import functools
import math

import jax
import jax.numpy as jnp
from jax import lax
from jax.experimental import pallas as pl
from jax.experimental.pallas import tpu as pltpu

F32 = jnp.float32
BF16 = jnp.bfloat16

D_MODEL = 1024
CHUNK = 64
POOL_WIDTH = 512
POOL_WINDOWS = (2, 4, 8, 16)
POOL_GROUP = 128
POOL_HIST = 15
HIST_ROWS = 16
N_HEADS = 4
HEAD_DIM = 64
HEAD_WIDTH = 128
QK_WIDTH = 512
PROJ_WIDTH = 2048
ATTN_SCALE = HEAD_DIM ** -0.5
LOG2_E = math.log2(math.e)
NEG_INF = -1e30
N_GROUPS = 4
EXPERTS_PER_GROUP = 4
N_EXPERTS = 16
D_EXPERT = 256
EPS = 1e-6
LANES = 128
KV_BLOCK = 256
Q_SUB = 128
ROUTER_LANE0 = N_GROUPS
ROUTE_TILE = 256
SEG_ALIGN = 16
ROW_W = D_MODEL + LANES
MOE_BLOCK = 512
ZERO_ROWS = 256
ZERO_FILL_ROWS = 2 * MOE_BLOCK
VMEM_LIMIT = 48 * 1024 * 1024


def _lambda_init(layer):
    return 0.8 - 0.6 * math.exp(-0.3 * layer)


def _inproj_kernel(x_ref, hist_ref, gmix_ref, win_ref, wvt_ref, wpool_ref, pscale_ref,
                   k_ref, v_ref, qb_ref, kb_ref, vb_ref, pool_ref, state_ref,
                   ext_ref, *, tt, start_pos, transposed_v):
    t = pl.program_id(1)
    x = x_ref[0]
    ms = jnp.mean(x * x, axis=-1, keepdims=True)
    h = (x * lax.rsqrt(ms + EPS) * gmix_ref[...]).astype(BF16)
    proj = jnp.dot(h, win_ref[...], preferred_element_type=F32)
    u = proj[:, :POOL_WIDTH]
    q = proj[:, POOL_WIDTH:POOL_WIDTH + QK_WIDTH]
    k = proj[:, POOL_WIDTH + QK_WIDTH:POOL_WIDTH + 2 * QK_WIDTH]
    v = proj[:, POOL_WIDTH + 2 * QK_WIDTH:]
    for hd in range(N_HEADS):
        k_ref[0, :, hd, :] = k[:, hd * HEAD_WIDTH:(hd + 1) * HEAD_WIDTH]
        v_ref[0, :, hd, :] = v[:, hd * HEAD_WIDTH:(hd + 1) * HEAD_WIDTH]
    qb_ref[0] = (q * (ATTN_SCALE * LOG2_E if transposed_v else ATTN_SCALE)).astype(BF16)
    kb_ref[0] = k.astype(BF16)
    if transposed_v:
        vt = lax.dot_general(wvt_ref[...], h, (((1,), (1,)), ((), ())),
                             preferred_element_type=F32).astype(BF16)
        for i in range(tt // KV_BLOCK):
            vb_ref[0, i] = vt[:, i * KV_BLOCK:(i + 1) * KV_BLOCK]
    else:
        vb_ref[0] = v.astype(BF16)

    @pl.when(t == 0)
    def _():
        ext_ref[0:HIST_ROWS] = hist_ref[0]

    @pl.when(t > 0)
    def _():
        ext_ref[0:HIST_ROWS] = ext_ref[tt:tt + HIST_ROWS]

    ext_ref[HIST_ROWS:HIST_ROWS + tt] = u

    pos = start_pos + t * tt + lax.broadcasted_iota(jnp.int32, (tt, 1), 0)
    outs = []
    for g, w in enumerate(POOL_WINDOWS):
        lo = g * POOL_GROUP
        ug = u[:, lo:lo + POOL_GROUP]
        acc = ug
        for j in range(1, w):
            acc = acc + ext_ref[HIST_ROWS - j:HIST_ROWS - j + tt, lo:lo + POOL_GROUP]
        inv_cnt = 1.0 / jnp.minimum(w, pos + 1).astype(F32)
        d = acc * inv_cnt - ug
        outs.append(jnp.dot(d.astype(BF16), wpool_ref[g], preferred_element_type=F32))
    y = jnp.concatenate(outs, axis=-1) * pscale_ref[...]
    pool_ref[0] = y.astype(BF16)

    @pl.when(t == pl.num_programs(1) - 1)
    def _():
        state_ref[0] = ext_ref[tt + 1:tt + HIST_ROWS]


def _inproj(x, hist16, g_mix, w_in, w_vt, w_pool, pool_scale, *, tt, start_pos, transposed_v):
    B, T, _ = x.shape
    nt = T // tt
    row = lambda b, t: (b, t, 0)
    const2 = lambda b, t: (0, 0)
    wide = lambda dt: jax.ShapeDtypeStruct((B, T, QK_WIDTH), dt)
    if transposed_v:
        vb_spec = pl.BlockSpec((1, tt // KV_BLOCK, QK_WIDTH, KV_BLOCK), lambda b, t: (b, t, 0, 0))
        vb_shape = jax.ShapeDtypeStruct((B, T // KV_BLOCK, QK_WIDTH, KV_BLOCK), BF16)
    else:
        vb_spec = pl.BlockSpec((1, tt, QK_WIDTH), row)
        vb_shape = wide(BF16)
    return pl.pallas_call(
        functools.partial(_inproj_kernel, tt=tt, start_pos=start_pos, transposed_v=transposed_v),
        grid=(B, nt),
        in_specs=[
            pl.BlockSpec((1, tt, D_MODEL), row),
            pl.BlockSpec((1, HIST_ROWS, POOL_WIDTH), lambda b, t: (b, 0, 0)),
            pl.BlockSpec((1, D_MODEL), const2),
            pl.BlockSpec((D_MODEL, PROJ_WIDTH), const2),
            pl.BlockSpec((QK_WIDTH, D_MODEL), const2),
            pl.BlockSpec((len(POOL_WINDOWS), POOL_GROUP, POOL_GROUP), lambda b, t: (0, 0, 0)),
            pl.BlockSpec((1, POOL_WIDTH), const2),
        ],
        out_specs=[
            pl.BlockSpec((1, tt, N_HEADS, HEAD_WIDTH), lambda b, t: (b, t, 0, 0)),
            pl.BlockSpec((1, tt, N_HEADS, HEAD_WIDTH), lambda b, t: (b, t, 0, 0)),
            pl.BlockSpec((1, tt, QK_WIDTH), row),
            pl.BlockSpec((1, tt, QK_WIDTH), row),
            vb_spec,
            pl.BlockSpec((1, tt, POOL_WIDTH), row),
            pl.BlockSpec((1, POOL_HIST, POOL_WIDTH), lambda b, t: (b, 0, 0)),
        ],
        out_shape=[jax.ShapeDtypeStruct((B, T, N_HEADS, HEAD_WIDTH), F32),
                   jax.ShapeDtypeStruct((B, T, N_HEADS, HEAD_WIDTH), F32),
                   wide(BF16), wide(BF16), vb_shape, wide(BF16),
                   jax.ShapeDtypeStruct((B, POOL_HIST, POOL_WIDTH), F32)],
        scratch_shapes=[pltpu.VMEM((HIST_ROWS + tt, POOL_WIDTH), F32)],
        compiler_params=pltpu.CompilerParams(
            dimension_semantics=("parallel", "arbitrary"), vmem_limit_bytes=VMEM_LIMIT),
        name="inproj_pool",
    )(x, hist16, g_mix, w_in, w_vt, w_pool, pool_scale)


def _lambda_value(lq1_ref, lk1_ref, lq2_ref, lk2_ref, li):
    s1 = jnp.sum(lq1_ref[...] * lk1_ref[...], axis=-1, keepdims=True)
    s2 = jnp.sum(lq2_ref[...] * lk2_ref[...], axis=-1, keepdims=True)
    return jnp.exp(s1) - jnp.exp(s2) + li


def _split_halves(q):
    lane = lax.broadcasted_iota(jnp.int32, (1, HEAD_WIDTH), 1)
    zero = jnp.zeros_like(q)
    return jnp.where(lane < HEAD_DIM, q, zero), jnp.where(lane >= HEAD_DIM, q, zero)


def _scores(qh, kblk):
    return lax.dot_general(qh, kblk, (((1,), (1,)), ((), ())), preferred_element_type=F32)


def _subln(o, gsub_ref, li):
    return o * lax.rsqrt(jnp.mean(o * o, axis=-1, keepdims=True) + EPS) * gsub_ref[...] * (1.0 - li)


def _attn_prompt_kernel(lq1_ref, lk1_ref, lq2_ref, lk2_ref, gsub_ref, bias_ref, q_ref, k_ref, vt_ref, o_ref,
                        m_ref, l_ref, acc_ref, *, tq, li):
    tk = KV_BLOCK
    qi = pl.program_id(2)
    q_start = qi * tq
    n_full = q_start // tk
    n_sub = tq // Q_SUB
    q2 = []
    for a in range(n_sub):
        q0, q1 = _split_halves(q_ref[0, a * Q_SUB:(a + 1) * Q_SUB, :])
        q2.append(jnp.concatenate([q0, q1], axis=0))
    m_ref[...] = jnp.full(m_ref.shape, NEG_INF, F32)
    l_ref[...] = jnp.zeros(l_ref.shape, F32)
    acc_ref[...] = jnp.zeros(acc_ref.shape, F32)

    def qk(j, masked):
        kblk = k_ref[0, pl.ds(pl.multiple_of(j * tk, tk), tk), :]
        out = []
        for a in range(n_sub):
            s = _scores(kblk, q2[a])
            if masked:
                s = s + bias_ref[a]
            out.append((s, jnp.max(s, axis=0, keepdims=True)))
        return tuple(out)

    ones_rows = jnp.ones((SEG_ALIGN, tk), BF16)

    def consume(j, sc):
        vtb = jnp.concatenate([vt_ref[0, j], ones_rows], axis=0)
        for a, (s, cmax) in enumerate(sc):
            m_old = m_ref[a]
            m_new = jnp.maximum(m_old, cmax)
            alpha = jnp.exp2(m_old - m_new)
            p = jnp.exp2(s - m_new)
            pv = jnp.dot(vtb, p.astype(BF16), preferred_element_type=F32)
            l_ref[a] = alpha * l_ref[a] + pv[HEAD_WIDTH:HEAD_WIDTH + 1]
            acc_ref[a] = alpha * acc_ref[a] + pv[:HEAD_WIDTH]
            m_ref[a] = m_new

    def body(j, carry):
        prev, sc = carry
        nxt = qk(j, False)
        consume(prev, sc)
        return j, nxt

    last, sc = lax.fori_loop(0, n_full, body, (n_full, qk(n_full, True)))
    consume(last, sc)
    lam = _lambda_value(lq1_ref, lk1_ref, lq2_ref, lk2_ref, li)
    scale = gsub_ref[...] * (1.0 - li)
    for a in range(n_sub):
        o2 = acc_ref[a] / l_ref[a]
        o_t = o2[:, :Q_SUB] - lam * o2[:, Q_SUB:]
        o_t = o_t * lax.rsqrt(jnp.mean(o_t * o_t, axis=0, keepdims=True) + EPS)
        o_ref[0, a * Q_SUB:(a + 1) * Q_SUB, :] = (o_t.T * scale).astype(BF16)


def _attn_prompt(lams, g_subln, qb, kb, vt, *, tq, li):
    B, T, _ = qb.shape
    assert tq == KV_BLOCK and tq % Q_SUB == 0
    key_chunk = jnp.arange(KV_BLOCK)[None, :, None] // CHUNK
    query = jnp.arange(tq // Q_SUB)[:, None, None] * Q_SUB + jnp.arange(2 * Q_SUB)[None, None, :] % Q_SUB
    bias = jnp.where(key_chunk <= query // CHUNK, 0.0, NEG_INF).astype(F32)
    c2 = lambda b, h, i: (0, 0)
    small = pl.BlockSpec((1, HEAD_DIM), c2)
    return pl.pallas_call(
        functools.partial(_attn_prompt_kernel, tq=tq, li=li),
        grid=(B, N_HEADS, T // tq),
        in_specs=[small, small, small, small,
                  pl.BlockSpec((1, HEAD_WIDTH), c2),
                  pl.BlockSpec(bias.shape, lambda b, h, i: (0, 0, 0)),
                  pl.BlockSpec((1, tq, HEAD_WIDTH), lambda b, h, i: (b, i, h)),
                  pl.BlockSpec((1, T, HEAD_WIDTH), lambda b, h, i: (b, 0, h)),
                  pl.BlockSpec((1, T // KV_BLOCK, HEAD_WIDTH, KV_BLOCK), lambda b, h, i: (b, 0, h, 0))],
        out_specs=pl.BlockSpec((1, tq, HEAD_WIDTH), lambda b, h, i: (b, i, h)),
        out_shape=jax.ShapeDtypeStruct((B, T, QK_WIDTH), BF16),
        scratch_shapes=[pltpu.VMEM((tq // Q_SUB, 1, 2 * Q_SUB), F32),
                        pltpu.VMEM((tq // Q_SUB, 1, 2 * Q_SUB), F32),
                        pltpu.VMEM((tq // Q_SUB, HEAD_WIDTH, 2 * Q_SUB), F32)],
        compiler_params=pltpu.CompilerParams(
            dimension_semantics=("parallel", "parallel", "arbitrary"), vmem_limit_bytes=VMEM_LIMIT),
        name="attn_prompt",
    )(*lams, g_subln, bias, qb, kb, vt)


def _attn_sample_kernel(lq1_ref, lk1_ref, lq2_ref, lk2_ref, gsub_ref, q_ref, kc_ref, vc_ref,
                        kn_ref, vn_ref, o_ref, *, li):
    q0, q1 = _split_halves(q_ref[0])
    kc = kc_ref[0]
    vc = vc_ref[0]
    kn = kn_ref[0]
    vn = vn_ref[0]
    outs = []
    for qh in (q0, q1):
        sc = _scores(qh, kc)
        sn = _scores(qh, kn)
        m = jnp.maximum(jnp.max(sc, axis=-1, keepdims=True), jnp.max(sn, axis=-1, keepdims=True))
        pc = jnp.exp(sc - m)
        pn = jnp.exp(sn - m)
        l = jnp.sum(pc, axis=-1, keepdims=True) + jnp.sum(pn, axis=-1, keepdims=True)
        acc = (jnp.dot(pc.astype(BF16), vc, preferred_element_type=F32)
               + jnp.dot(pn.astype(BF16), vn, preferred_element_type=F32))
        outs.append(acc / l)
    lam = _lambda_value(lq1_ref, lk1_ref, lq2_ref, lk2_ref, li)
    o = outs[0] - lam * outs[1]
    o_ref[0] = _subln(o, gsub_ref, li).astype(BF16)


def _attn_sample(lams, g_subln, qb, cache_k, cache_v, kb, vb, *, li):
    B, T, _ = qb.shape
    P = cache_k.shape[1]
    c2 = lambda b, h: (0, 0)
    small = pl.BlockSpec((1, HEAD_DIM), c2)
    head = lambda rows: pl.BlockSpec((1, rows, HEAD_WIDTH), lambda b, h: (b, 0, h))
    return pl.pallas_call(
        functools.partial(_attn_sample_kernel, li=li),
        grid=(B, N_HEADS),
        in_specs=[small, small, small, small, pl.BlockSpec((1, HEAD_WIDTH), c2),
                  head(T), head(P), head(P), head(T), head(T)],
        out_specs=head(T),
        out_shape=jax.ShapeDtypeStruct((B, T, QK_WIDTH), BF16),
        compiler_params=pltpu.CompilerParams(
            dimension_semantics=("parallel", "parallel"), vmem_limit_bytes=VMEM_LIMIT),
        name="attn_sample",
    )(*lams, g_subln, qb, cache_k, cache_v, kb, vb)


def _lane_first_argmax(vals, lane):
    vmax = jnp.max(vals, axis=-1, keepdims=True)
    idx = jnp.min(jnp.where(vals == vmax, lane, LANES), axis=-1, keepdims=True)
    return vmax, idx


def _bf16_split(x):
    hi = x.astype(BF16)
    return hi, (x - hi.astype(F32)).astype(BF16)


def _route_sort_kernel(x_ref, pool_ref, attn_ref, wout_ref, gffn_ref, wr_hi_ref, wr_lo_ref, br_ref,
                       x1_ref, route_ref, seg_ref, hs_ref,
                       stage_ref, zero_ref, run_ref, sem_ref, *, tm, cap, ntiles):
    i = pl.program_id(0)
    slot = lax.rem(i, 2)
    slots = tm + N_GROUPS * SEG_ALIGN

    @pl.when(i == 0)
    def _():
        for g in range(N_GROUPS):
            run_ref[g] = 0
        stage_ref[...] = jnp.zeros(stage_ref.shape, BF16)
        zero_ref[...] = jnp.zeros(zero_ref.shape, BF16)

    mix = (jnp.dot(pool_ref[...], wout_ref[0:POOL_WIDTH, :], preferred_element_type=F32)
           + jnp.dot(attn_ref[...], wout_ref[POOL_WIDTH:, :], preferred_element_type=F32))
    x1 = x_ref[...] + mix
    x1_ref[...] = x1
    ms = jnp.mean(x1 * x1, axis=-1, keepdims=True)
    h2 = x1 * lax.rsqrt(ms + EPS) * gffn_ref[...]
    h_hi, h_lo = _bf16_split(h2)
    logits = (jnp.dot(h_hi, wr_hi_ref[...], preferred_element_type=F32)
              + jnp.dot(h_hi, wr_lo_ref[...], preferred_element_type=F32)
              + jnp.dot(h_lo, wr_hi_ref[...], preferred_element_type=F32)
              + br_ref[...])
    lane = lax.broadcasted_iota(jnp.int32, logits.shape, 1)
    neg = jnp.float32(-jnp.inf)
    gl = jnp.where(lane < N_GROUPS, logits, neg)
    gmax, g_idx = _lane_first_argmax(gl, lane)
    g_gate = 1.0 / jnp.sum(jnp.exp(gl - gmax), axis=-1, keepdims=True)
    e_lo = ROUTER_LANE0 + EXPERTS_PER_GROUP * g_idx
    el = jnp.where((lane >= e_lo) & (lane < e_lo + EXPERTS_PER_GROUP), logits, neg)
    v1, i1 = _lane_first_argmax(el, lane)
    v2, i2 = _lane_first_argmax(jnp.where(lane == i1, neg, el), lane)
    e21 = jnp.exp(v2 - v1)
    w1 = g_gate / (1.0 + e21)
    w2 = w1 * e21
    k1 = i1 - e_lo
    k2 = i2 - e_lo
    w1h = w1.astype(BF16).astype(F32)
    w2h = w2.astype(BF16).astype(F32)
    cparts = (jnp.where(lane == k1, w1h, 0.0) + jnp.where(lane == k2, w2h, 0.0)
              + jnp.where(lane == k1 + EXPERTS_PER_GROUP, w1 - w1h, 0.0)
              + jnp.where(lane == k2 + EXPERTS_PER_GROUP, w2 - w2h, 0.0))

    onehot = jnp.where(lane == g_idx, 1.0, 0.0)
    before = (lax.broadcasted_iota(jnp.int32, (tm, tm), 1)
              < lax.broadcasted_iota(jnp.int32, (tm, tm), 0))
    rank_all = jnp.dot(jnp.where(before, 1.0, 0.0).astype(BF16), onehot.astype(BF16),
                       preferred_element_type=F32)
    n = jnp.sum(onehot, axis=0, keepdims=True)
    n_pad = jnp.floor((n + (SEG_ALIGN - 1)) * (1.0 / SEG_ALIGN)) * SEG_ALIGN
    upper = (lax.broadcasted_iota(jnp.int32, (LANES, LANES), 0)
             < lax.broadcasted_iota(jnp.int32, (LANES, LANES), 1))
    off = jnp.dot(jnp.broadcast_to(n_pad, (8, LANES)).astype(BF16), jnp.where(upper, 1.0, 0.0).astype(BF16),
                  preferred_element_type=F32)[0:1]
    rank = jnp.sum(onehot * rank_all, axis=-1, keepdims=True)
    local = jnp.sum(onehot * off, axis=-1, keepdims=True) + rank
    route_ref[...] = jnp.broadcast_to(g_idx.astype(F32) * tm + rank, (tm, LANES))

    place = jnp.where(lax.broadcasted_iota(jnp.int32, (tm, slots), 1).astype(F32) == local, 1.0, 0.0)
    rows = jnp.concatenate([h_hi, cparts.astype(BF16)], axis=1)
    sorted_rows = lax.dot_general(place.astype(BF16), rows, (((0,), (0,)), ((), ())),
                                  preferred_element_type=F32)
    stage_ref[slot, 0:slots, :] = sorted_rows.astype(BF16)

    def window(g, slot_, src_row, dst_row):
        return pltpu.make_async_copy(
            stage_ref.at[slot_, pl.ds(pl.multiple_of(src_row, SEG_ALIGN), tm)],
            hs_ref.at[pl.ds(pl.multiple_of(dst_row, SEG_ALIGN), tm)],
            sem_ref.at[slot_])

    @pl.when(i > 0)
    def _():
        for g in range(N_GROUPS):
            window(g, 1 - slot, 0, 0).wait()

    for g in range(N_GROUPS):
        src = off[0, g].astype(jnp.int32)
        cnt = n_pad[0, g].astype(jnp.int32)
        start = run_ref[g]
        seg_ref[g, i] = start
        window(g, slot, src, g * cap + start).start()
        run_ref[g] = start + cnt

    @pl.when(i == ntiles - 1)
    def _():
        for g in range(N_GROUPS):
            window(g, slot, 0, 0).wait()
        fills = []
        for g in range(N_GROUPS):
            total = run_ref[g]
            seg_ref[g, ntiles] = total
            for z in range(ZERO_FILL_ROWS // ZERO_ROWS):
                dst = g * cap + total + z * ZERO_ROWS
                fills.append(pltpu.make_async_copy(
                    zero_ref, hs_ref.at[pl.ds(pl.multiple_of(dst, SEG_ALIGN), ZERO_ROWS)],
                    sem_ref.at[2]))
        for f in fills:
            f.start()
        for f in fills:
            f.wait()


def _round_up(x, m):
    return (x + m - 1) // m * m


def _sorted_capacity(n_tokens, ntiles):
    return _round_up(n_tokens + SEG_ALIGN * ntiles, MOE_BLOCK) + 3 * MOE_BLOCK


def _route_sort(x2d, pool2d, attn2d, w_out, g_ffn, wr_hi, wr_lo, br, *, tm):
    N = x2d.shape[0]
    ntiles = N // tm
    cap = _sorted_capacity(N, ntiles)
    row = lambda i: (i, 0)
    c2 = lambda i: (0, 0)
    return pl.pallas_call(
        functools.partial(_route_sort_kernel, tm=tm, cap=cap, ntiles=ntiles),
        grid=(ntiles,),
        in_specs=[pl.BlockSpec((tm, D_MODEL), row),
                  pl.BlockSpec((tm, POOL_WIDTH), row),
                  pl.BlockSpec((tm, QK_WIDTH), row),
                  pl.BlockSpec((D_MODEL, D_MODEL), c2),
                  pl.BlockSpec((1, D_MODEL), c2),
                  pl.BlockSpec((D_MODEL, LANES), c2),
                  pl.BlockSpec((D_MODEL, LANES), c2),
                  pl.BlockSpec((1, LANES), c2)],
        out_specs=[pl.BlockSpec((tm, D_MODEL), row),
                   pl.BlockSpec((tm, LANES), row),
                   pl.BlockSpec(memory_space=pltpu.SMEM),
                   pl.BlockSpec(memory_space=pl.ANY)],
        out_shape=[jax.ShapeDtypeStruct((N, D_MODEL), F32),
                   jax.ShapeDtypeStruct((N, LANES), F32),
                   jax.ShapeDtypeStruct((N_GROUPS, ntiles + 1), jnp.int32),
                   jax.ShapeDtypeStruct((N_GROUPS * cap, ROW_W), BF16)],
        scratch_shapes=[pltpu.VMEM((2, 2 * tm + N_GROUPS * SEG_ALIGN, ROW_W), BF16),
                        pltpu.VMEM((ZERO_ROWS, ROW_W), BF16),
                        pltpu.SMEM((N_GROUPS,), jnp.int32),
                        pltpu.SemaphoreType.DMA((3,))],
        compiler_params=pltpu.CompilerParams(
            dimension_semantics=("arbitrary",), vmem_limit_bytes=VMEM_LIMIT),
        name="route_sort",
    )(x2d, pool2d, attn2d, w_out, g_ffn, wr_hi, wr_lo, br)


def _group_moe_kernel(blk_ref, grp_ref, valid_ref, hs_ref, wg_ref, wu_ref, wd_ref, out_ref):
    b = pl.program_id(0)

    @pl.when(valid_ref[b] == 1)
    def _():
        rows = hs_ref[...]
        x = rows[:, :D_MODEL]
        cparts = rows[:, D_MODEL:].astype(F32)
        lane = lax.broadcasted_iota(jnp.int32, cparts.shape, 1)
        hes = []
        for e in range(EXPERTS_PER_GROUP):
            a = jnp.dot(x, wg_ref[e], preferred_element_type=F32)
            u = jnp.dot(x, wu_ref[e], preferred_element_type=F32)
            c = jnp.sum(jnp.where((lane == e) | (lane == e + EXPERTS_PER_GROUP), cparts, 0.0),
                        axis=-1, keepdims=True)
            hes.append((a * jax.nn.sigmoid(a) * u * c).astype(BF16))
        he = jnp.concatenate(hes, axis=1)
        out_ref[...] = jnp.dot(he, wd_ref[0], preferred_element_type=F32).astype(BF16)

    @pl.when(valid_ref[b] == 0)
    def _():
        out_ref[...] = jnp.zeros(out_ref.shape, BF16)


def _group_moe(blk, grp, valid, hs, w_gate, w_up, w_down_grouped):
    nb = blk.shape[0]
    bm = MOE_BLOCK
    by_block = lambda b, blk, grp, valid: (blk[b], 0)
    by_group = lambda b, blk, grp, valid: (grp[b], 0, 0)
    grid_spec = pltpu.PrefetchScalarGridSpec(
        num_scalar_prefetch=3,
        grid=(nb,),
        in_specs=[pl.BlockSpec((bm, ROW_W), by_block),
                  pl.BlockSpec((EXPERTS_PER_GROUP, D_MODEL, D_EXPERT), by_group),
                  pl.BlockSpec((EXPERTS_PER_GROUP, D_MODEL, D_EXPERT), by_group),
                  pl.BlockSpec((1, EXPERTS_PER_GROUP * D_EXPERT, D_MODEL), by_group)],
        out_specs=pl.BlockSpec((bm, D_MODEL), by_block),
    )
    return pl.pallas_call(
        _group_moe_kernel,
        grid_spec=grid_spec,
        out_shape=jax.ShapeDtypeStruct((hs.shape[0], D_MODEL), BF16),
        compiler_params=pltpu.CompilerParams(
            dimension_semantics=("arbitrary",), vmem_limit_bytes=VMEM_LIMIT),
        name="group_moe",
    )(blk, grp, valid, hs, w_gate, w_up, w_down_grouped)


def _block_table(seg, n_tokens, ntiles, cap):
    bm = MOE_BLOCK
    nb_max = _round_up(n_tokens + N_GROUPS * SEG_ALIGN * ntiles, bm) // bm + 2 * N_GROUPS
    total = seg[:, ntiles]
    nb = (total + bm - 1) // bm + 1
    end = jnp.cumsum(nb)
    start = end - nb
    b = jnp.arange(nb_max, dtype=jnp.int32)
    g = jnp.sum((b[:, None] >= end[None, :]).astype(jnp.int32), axis=1)
    valid = g < N_GROUPS
    gc = jnp.minimum(g, N_GROUPS - 1)
    blk = jnp.where(valid, gc * (cap // bm) + b - start[gc], N_GROUPS * cap // bm - 1)
    return blk.astype(jnp.int32), gc.astype(jnp.int32), valid.astype(jnp.int32)


def _unsort_norm_kernel(seg_ref, x1_ref, route_ref, gfin_ref, os_ref, y_ref, win_ref, sem_ref,
                        *, tm, cap, ntiles):
    i = pl.program_id(0)
    slot = lax.rem(i, 2)

    def window(tile, g, slot_):
        start = seg_ref[g * (ntiles + 1) + tile]
        return pltpu.make_async_copy(
            os_ref.at[pl.ds(pl.multiple_of(g * cap + start, SEG_ALIGN), tm)],
            win_ref.at[slot_, g], sem_ref.at[slot_])

    @pl.when(i == 0)
    def _():
        for g in range(N_GROUPS):
            window(0, g, 0).start()

    @pl.when(i + 1 < ntiles)
    def _():
        for g in range(N_GROUPS):
            window(i + 1, g, 1 - slot).start()

    for g in range(N_GROUPS):
        window(i, g, slot).wait()

    where = route_ref[:, 0:1]
    pick = jnp.where(lax.broadcasted_iota(jnp.int32, (tm, N_GROUPS * tm), 1).astype(F32) == where, 1.0, 0.0)
    moe = jnp.dot(pick.astype(BF16), win_ref[slot].reshape(N_GROUPS * tm, D_MODEL),
                  preferred_element_type=F32)
    x2 = x1_ref[...] + moe
    ms = jnp.mean(x2 * x2, axis=-1, keepdims=True)
    y_ref[...] = x2 * lax.rsqrt(ms + EPS) * gfin_ref[...]


def _unsort_norm(seg, x1, route, g_final, out_sorted, *, tm, cap):
    N = x1.shape[0]
    ntiles = N // tm
    grid_spec = pltpu.PrefetchScalarGridSpec(
        num_scalar_prefetch=1,
        grid=(ntiles,),
        in_specs=[pl.BlockSpec((tm, D_MODEL), lambda i, seg: (i, 0)),
                  pl.BlockSpec((tm, LANES), lambda i, seg: (i, 0)),
                  pl.BlockSpec((1, D_MODEL), lambda i, seg: (0, 0)),
                  pl.BlockSpec(memory_space=pl.ANY)],
        out_specs=pl.BlockSpec((tm, D_MODEL), lambda i, seg: (i, 0)),
        scratch_shapes=[pltpu.VMEM((2, N_GROUPS, tm, D_MODEL), BF16),
                        pltpu.SemaphoreType.DMA((2,))],
    )
    return pl.pallas_call(
        functools.partial(_unsort_norm_kernel, tm=tm, cap=cap, ntiles=ntiles),
        grid_spec=grid_spec,
        out_shape=jax.ShapeDtypeStruct((N, D_MODEL), F32),
        compiler_params=pltpu.CompilerParams(
            dimension_semantics=("arbitrary",), vmem_limit_bytes=VMEM_LIMIT),
        name="unsort_norm",
    )(seg.reshape(-1), x1, route, g_final, out_sorted)


def _layer(x, hist16, cache, start_pos, layer, w, g_final, *, tt, tm):
    B, T, _ = x.shape
    li = _lambda_init(layer)
    k, v, qb, kb, vb, pool, state = _inproj(x, hist16, w["g_mix"], w["w_in"], w["w_vt"], w["w_pool"],
                                            w["pool_scale"], tt=tt, start_pos=start_pos,
                                            transposed_v=cache is None)
    if cache is None:
        attn = _attn_prompt(w["lams"], w["g_subln"], qb, kb, vb, tq=256, li=li)
    else:
        attn = _attn_sample(w["lams"], w["g_subln"], qb, cache[0], cache[1], kb, vb, li=li)
    N = B * T
    ntiles = N // tm
    cap = _sorted_capacity(N, ntiles)
    x1, route, seg, hs = _route_sort(x.reshape(N, D_MODEL), pool.reshape(N, POOL_WIDTH),
                                     attn.reshape(N, QK_WIDTH), w["w_out"], w["g_ffn"],
                                     w["wr_hi"], w["wr_lo"], w["br"], tm=tm)
    blk, grp, valid = _block_table(seg, N, ntiles, cap)
    out_sorted = _group_moe(blk, grp, valid, hs, w["w_gate"], w["w_up"], w["w_down"])
    y = _unsort_norm(seg, x1, route, g_final, out_sorted, tm=tm, cap=cap)
    return y.reshape(B, T, D_MODEL), k, v, state


def kernel(x_prompt, x_sample, cache_k, cache_v, state_pool, g_mix, w_in, w_pool, pool_scale, lam_q1, lam_k1, lam_q2, lam_k2, g_subln, w_out, g_ffn, w_group, b_group, w_erouter, b_erouter, w_gate, w_up, w_down, g_final):
    depth = g_mix.shape[0]
    assert depth == 1, "the final norm is fused into the layer, so exactly one layer is supported"
    l = 0
    wr = jnp.concatenate([w_group[l], w_erouter[l]], axis=1)
    wr = jnp.pad(wr, ((0, 0), (0, LANES - wr.shape[1])))
    wr_hi = wr.astype(BF16)
    wr_lo = (wr - wr_hi.astype(F32)).astype(BF16)
    br = jnp.pad(jnp.concatenate([b_group[l], b_erouter[l]]), (0, LANES - N_GROUPS - N_EXPERTS))
    w = dict(
        g_mix=g_mix[l][None], w_in=w_in[l].astype(BF16),
        w_vt=w_in[l][:, POOL_WIDTH + 2 * QK_WIDTH:].T.astype(BF16), w_pool=w_pool[l].astype(BF16),
        pool_scale=pool_scale[l][None],
        lams=(lam_q1[l][None], lam_k1[l][None], lam_q2[l][None], lam_k2[l][None]),
        g_subln=g_subln[l][None], w_out=w_out[l].astype(BF16), g_ffn=g_ffn[l][None],
        wr_hi=wr_hi, wr_lo=wr_lo, br=br[None],
        w_gate=w_gate[l].astype(BF16), w_up=w_up[l].astype(BF16),
        w_down=w_down[l].astype(BF16).reshape(N_GROUPS, EXPERTS_PER_GROUP * D_EXPERT, D_MODEL),
    )
    gfin = g_final[None]
    Bp = x_prompt.shape[0]
    Bs, _, _ = x_sample.shape
    P = cache_k.shape[2]
    zero_hist = jnp.zeros((Bp, HIST_ROWS, POOL_WIDTH), F32)
    samp_hist = jnp.pad(state_pool[l], ((0, 0), (HIST_ROWS - POOL_HIST, 0), (0, 0)))
    cache = (cache_k[l].reshape(Bs, P, QK_WIDTH).astype(BF16),
             cache_v[l].reshape(Bs, P, QK_WIDTH).astype(BF16))

    yp, kp, vp, pp = _layer(x_prompt, zero_hist, None, 0, l, w, gfin, tt=512, tm=ROUTE_TILE)
    ys, kn, vn, pn = _layer(x_sample, samp_hist, cache, P, l, w, gfin, tt=64, tm=ROUTE_TILE)
    return (yp, ys, kp[None], vp[None], pp[None], kn[None], vn[None], pn[None])
```

```python
import functools
import math

import jax
import jax.numpy as jnp
from jax import lax
from jax.experimental import pallas as pl
from jax.experimental.pallas import tpu as pltpu

F32 = jnp.float32
BF16 = jnp.bfloat16

D_MODEL = 1024
CHUNK = 64
POOL_WIDTH = 512
POOL_WINDOWS = (2, 4, 8, 16)
POOL_GROUP = 128
POOL_HIST = 15
HIST_ROWS = 16
N_HEADS = 4
HEAD_DIM = 64
HEAD_WIDTH = 128
QK_WIDTH = 512
PROJ_WIDTH = 2048
ATTN_SCALE = HEAD_DIM ** -0.5
LOG2_E = math.log2(math.e)
NEG_INF = -1e30
N_GROUPS = 4
EXPERTS_PER_GROUP = 4
N_EXPERTS = 16
D_EXPERT = 256
EPS = 1e-6
LANES = 128
KV_BLOCK = 256
Q_SUB = 128
ROUTER_LANE0 = N_GROUPS
ROUTE_TILE = 256
SEG_ALIGN = 16
ROW_W = D_MODEL + LANES
MOE_BLOCK = 512
ZERO_ROWS = 256
ZERO_FILL_ROWS = 2 * MOE_BLOCK
VMEM_LIMIT = 48 * 1024 * 1024


def _lambda_init(layer):
    return 0.8 - 0.6 * math.exp(-0.3 * layer)


def _inproj_kernel(x_ref, hist_ref, gmix_ref, win_ref, wvt_ref, wpool_ref, pscale_ref,
                   k_ref, v_ref, qb_ref, kb_ref, vb_ref, pool_ref, state_ref,
                   ext_ref, *, tt, start_pos, transposed_v):
    t = pl.program_id(1)
    x = x_ref[0]
    ms = jnp.mean(x * x, axis=-1, keepdims=True)
    h = (x * lax.rsqrt(ms + EPS) * gmix_ref[...]).astype(BF16)
    proj = jnp.dot(h, win_ref[...], preferred_element_type=F32)
    u = proj[:, :POOL_WIDTH]
    q = proj[:, POOL_WIDTH:POOL_WIDTH + QK_WIDTH]
    k = proj[:, POOL_WIDTH + QK_WIDTH:POOL_WIDTH + 2 * QK_WIDTH]
    v = proj[:, POOL_WIDTH + 2 * QK_WIDTH:]
    for hd in range(N_HEADS):
        k_ref[0, :, hd, :] = k[:, hd * HEAD_WIDTH:(hd + 1) * HEAD_WIDTH]
        v_ref[0, :, hd, :] = v[:, hd * HEAD_WIDTH:(hd + 1) * HEAD_WIDTH]
    qb_ref[0] = (q * (ATTN_SCALE * LOG2_E if transposed_v else ATTN_SCALE)).astype(BF16)
    kb_ref[0] = k.astype(BF16)
    if transposed_v:
        vt = lax.dot_general(wvt_ref[...], h, (((1,), (1,)), ((), ())),
                             preferred_element_type=F32).astype(BF16)
        for i in range(tt // KV_BLOCK):
            vb_ref[0, i] = vt[:, i * KV_BLOCK:(i + 1) * KV_BLOCK]
    else:
        vb_ref[0] = v.astype(BF16)

    @pl.when(t == 0)
    def _():
        ext_ref[0:HIST_ROWS] = hist_ref[0]

    @pl.when(t > 0)
    def _():
        ext_ref[0:HIST_ROWS] = ext_ref[tt:tt + HIST_ROWS]

    ext_ref[HIST_ROWS:HIST_ROWS + tt] = u

    pos = start_pos + t * tt + lax.broadcasted_iota(jnp.int32, (tt, 1), 0)
    outs = []
    for g, w in enumerate(POOL_WINDOWS):
        lo = g * POOL_GROUP
        ug = u[:, lo:lo + POOL_GROUP]
        acc = ug
        for j in range(1, w):
            acc = acc + ext_ref[HIST_ROWS - j:HIST_ROWS - j + tt, lo:lo + POOL_GROUP]
        inv_cnt = 1.0 / jnp.minimum(w, pos + 1).astype(F32)
        d = acc * inv_cnt - ug
        outs.append(jnp.dot(d.astype(BF16), wpool_ref[g], preferred_element_type=F32))
    y = jnp.concatenate(outs, axis=-1) * pscale_ref[...]
    pool_ref[0] = y.astype(BF16)

    @pl.when(t == pl.num_programs(1) - 1)
    def _():
        state_ref[0] = ext_ref[tt + 1:tt + HIST_ROWS]


def _inproj(x, hist16, g_mix, w_in, w_vt, w_pool, pool_scale, *, tt, start_pos, transposed_v):
    B, T, _ = x.shape
    nt = T // tt
    row = lambda b, t: (b, t, 0)
    const2 = lambda b, t: (0, 0)
    wide = lambda dt: jax.ShapeDtypeStruct((B, T, QK_WIDTH), dt)
    if transposed_v:
        vb_spec = pl.BlockSpec((1, tt // KV_BLOCK, QK_WIDTH, KV_BLOCK), lambda b, t: (b, t, 0, 0))
        vb_shape = jax.ShapeDtypeStruct((B, T // KV_BLOCK, QK_WIDTH, KV_BLOCK), BF16)
    else:
        vb_spec = pl.BlockSpec((1, tt, QK_WIDTH), row)
        vb_shape = wide(BF16)
    return pl.pallas_call(
        functools.partial(_inproj_kernel, tt=tt, start_pos=start_pos, transposed_v=transposed_v),
        grid=(B, nt),
        in_specs=[
            pl.BlockSpec((1, tt, D_MODEL), row),
            pl.BlockSpec((1, HIST_ROWS, POOL_WIDTH), lambda b, t: (b, 0, 0)),
            pl.BlockSpec((1, D_MODEL), const2),
            pl.BlockSpec((D_MODEL, PROJ_WIDTH), const2),
            pl.BlockSpec((QK_WIDTH, D_MODEL), const2),
            pl.BlockSpec((len(POOL_WINDOWS), POOL_GROUP, POOL_GROUP), lambda b, t: (0, 0, 0)),
            pl.BlockSpec((1, POOL_WIDTH), const2),
        ],
        out_specs=[
            pl.BlockSpec((1, tt, N_HEADS, HEAD_WIDTH), lambda b, t: (b, t, 0, 0)),
            pl.BlockSpec((1, tt, N_HEADS, HEAD_WIDTH), lambda b, t: (b, t, 0, 0)),
            pl.BlockSpec((1, tt, QK_WIDTH), row),
            pl.BlockSpec((1, tt, QK_WIDTH), row),
            vb_spec,
            pl.BlockSpec((1, tt, POOL_WIDTH), row),
            pl.BlockSpec((1, POOL_HIST, POOL_WIDTH), lambda b, t: (b, 0, 0)),
        ],
        out_shape=[jax.ShapeDtypeStruct((B, T, N_HEADS, HEAD_WIDTH), F32),
                   jax.ShapeDtypeStruct((B, T, N_HEADS, HEAD_WIDTH), F32),
                   wide(BF16), wide(BF16), vb_shape, wide(BF16),
                   jax.ShapeDtypeStruct((B, POOL_HIST, POOL_WIDTH), F32)],
        scratch_shapes=[pltpu.VMEM((HIST_ROWS + tt, POOL_WIDTH), F32)],
        compiler_params=pltpu.CompilerParams(
            dimension_semantics=("parallel", "arbitrary"), vmem_limit_bytes=VMEM_LIMIT),
        name="inproj_pool",
    )(x, hist16, g_mix, w_in, w_vt, w_pool, pool_scale)


def _lambda_value(lq1_ref, lk1_ref, lq2_ref, lk2_ref, li):
    s1 = jnp.sum(lq1_ref[...] * lk1_ref[...], axis=-1, keepdims=True)
    s2 = jnp.sum(lq2_ref[...] * lk2_ref[...], axis=-1, keepdims=True)
    return jnp.exp(s1) - jnp.exp(s2) + li


def _split_halves(q):
    lane = lax.broadcasted_iota(jnp.int32, (1, HEAD_WIDTH), 1)
    zero = jnp.zeros_like(q)
    return jnp.where(lane < HEAD_DIM, q, zero), jnp.where(lane >= HEAD_DIM, q, zero)


def _scores(qh, kblk):
    return lax.dot_general(qh, kblk, (((1,), (1,)), ((), ())), preferred_element_type=F32)


def _subln(o, gsub_ref, li):
    return o * lax.rsqrt(jnp.mean(o * o, axis=-1, keepdims=True) + EPS) * gsub_ref[...] * (1.0 - li)


def _attn_prompt_kernel(lq1_ref, lk1_ref, lq2_ref, lk2_ref, gsub_ref, bias_ref, q_ref, k_ref, vt_ref, o_ref,
                        m_ref, l_ref, acc_ref, *, tq, li):
    tk = KV_BLOCK
    n_sub = tq // Q_SUB
    n_tiles = q_ref.shape[1] // tq

    def query_maps(qi):
        q2 = []
        for a in range(n_sub):
            rows = q_ref[0, pl.ds(pl.multiple_of(qi * tq + a * Q_SUB, Q_SUB), Q_SUB), :]
            q0, q1 = _split_halves(rows)
            q2.append(jnp.concatenate([q0, q1], axis=0))
        return q2

    def qk(q2, j, masked):
        kblk = k_ref[0, pl.ds(pl.multiple_of(j * tk, tk), tk), :]
        out = []
        for a in range(n_sub):
            s = _scores(kblk, q2[a])
            if masked:
                s = s + bias_ref[a]
            out.append((s, jnp.max(s, axis=0, keepdims=True)))
        return tuple(out)

    ones_rows = jnp.ones((SEG_ALIGN, tk), BF16)

    def consume(j, sc):
        vtb = jnp.concatenate([vt_ref[0, j], ones_rows], axis=0)
        for a, (s, cmax) in enumerate(sc):
            m_old = m_ref[a]
            m_new = jnp.maximum(m_old, cmax)
            alpha = jnp.exp2(m_old - m_new)
            p = jnp.exp2(s - m_new)
            pv = jnp.dot(vtb, p.astype(BF16), preferred_element_type=F32)
            l_ref[a] = alpha * l_ref[a] + pv[HEAD_WIDTH:HEAD_WIDTH + 1]
            acc_ref[a] = alpha * acc_ref[a] + pv[:HEAD_WIDTH]
            m_ref[a] = m_new

    lam = _lambda_value(lq1_ref, lk1_ref, lq2_ref, lk2_ref, li)
    scale = gsub_ref[...] * (1.0 - li)

    def tile(qi, sc):
        q2 = query_maps(qi)
        m_ref[...] = jnp.full(m_ref.shape, NEG_INF, F32)
        l_ref[...] = jnp.zeros(l_ref.shape, F32)
        acc_ref[...] = jnp.zeros(acc_ref.shape, F32)

        def body(j, carry):
            prev, cur = carry
            nxt = qk(q2, j, False)
            consume(prev, cur)
            return j, nxt

        last, cur = lax.fori_loop(0, qi, body, (qi, sc))
        qn = jnp.minimum(qi + 1, n_tiles - 1)
        nxt = qk(query_maps(qn), qn, True)
        consume(last, cur)
        for a in range(n_sub):
            o2 = acc_ref[a] / l_ref[a]
            o_t = o2[:, :Q_SUB] - lam * o2[:, Q_SUB:]
            o_t = o_t * lax.rsqrt(jnp.mean(o_t * o_t, axis=0, keepdims=True) + EPS)
            rows = pl.ds(pl.multiple_of(qi * tq + a * Q_SUB, Q_SUB), Q_SUB)
            o_ref[0, rows, :] = (o_t.T * scale).astype(BF16)
        return nxt

    lax.fori_loop(0, n_tiles, tile, qk(query_maps(0), 0, True))


def _attn_prompt(lams, g_subln, qb, kb, vt, *, tq, li):
    B, T, _ = qb.shape
    assert tq == KV_BLOCK and tq % Q_SUB == 0
    key_chunk = jnp.arange(KV_BLOCK)[None, :, None] // CHUNK
    query = jnp.arange(tq // Q_SUB)[:, None, None] * Q_SUB + jnp.arange(2 * Q_SUB)[None, None, :] % Q_SUB
    bias = jnp.where(key_chunk <= query // CHUNK, 0.0, NEG_INF).astype(F32)
    c2 = lambda b, h: (0, 0)
    small = pl.BlockSpec((1, HEAD_DIM), c2)
    head = pl.BlockSpec((1, T, HEAD_WIDTH), lambda b, h: (b, 0, h))
    return pl.pallas_call(
        functools.partial(_attn_prompt_kernel, tq=tq, li=li),
        grid=(B, N_HEADS),
        in_specs=[small, small, small, small,
                  pl.BlockSpec((1, HEAD_WIDTH), c2),
                  pl.BlockSpec(bias.shape, lambda b, h: (0, 0, 0)),
                  head, head,
                  pl.BlockSpec((1, T // KV_BLOCK, HEAD_WIDTH, KV_BLOCK), lambda b, h: (b, 0, h, 0))],
        out_specs=head,
        out_shape=jax.ShapeDtypeStruct((B, T, QK_WIDTH), BF16),
        scratch_shapes=[pltpu.VMEM((tq // Q_SUB, 1, 2 * Q_SUB), F32),
                        pltpu.VMEM((tq // Q_SUB, 1, 2 * Q_SUB), F32),
                        pltpu.VMEM((tq // Q_SUB, HEAD_WIDTH, 2 * Q_SUB), F32)],
        compiler_params=pltpu.CompilerParams(
            dimension_semantics=("parallel", "parallel"), vmem_limit_bytes=VMEM_LIMIT),
        name="attn_prompt",
    )(*lams, g_subln, bias, qb, kb, vt)


def _attn_sample_kernel(lq1_ref, lk1_ref, lq2_ref, lk2_ref, gsub_ref, q_ref, kc_ref, vc_ref,
                        kn_ref, vn_ref, o_ref, *, li):
    T = q_ref.shape[1]
    lam = _lambda_value(lq1_ref, lk1_ref, lq2_ref, lk2_ref, li)
    for hd in range(N_HEADS):
        cols = slice(hd * HEAD_WIDTH, (hd + 1) * HEAD_WIDTH)
        q2 = jnp.concatenate(_split_halves(q_ref[0, :, cols]), axis=0)
        kc = kc_ref[0, :, hd, :].astype(BF16)
        vc = vc_ref[0, :, hd, :].astype(BF16)
        sc = _scores(q2, kc)
        sn = _scores(q2, kn_ref[0, :, cols])
        m = jnp.maximum(jnp.max(sc, axis=-1, keepdims=True), jnp.max(sn, axis=-1, keepdims=True))
        pc = jnp.exp(sc - m)
        pn = jnp.exp(sn - m)
        l = jnp.sum(pc, axis=-1, keepdims=True) + jnp.sum(pn, axis=-1, keepdims=True)
        acc = (jnp.dot(pc.astype(BF16), vc, preferred_element_type=F32)
               + jnp.dot(pn.astype(BF16), vn_ref[0, :, cols], preferred_element_type=F32))
        o2 = acc / l
        o = o2[:T] - lam * o2[T:]
        o_ref[0, :, cols] = _subln(o, gsub_ref, li).astype(BF16)


def _attn_sample(lams, g_subln, qb, cache_k, cache_v, kb, vb, *, li):
    B, T, _ = qb.shape
    P = cache_k.shape[1]
    c2 = lambda b: (0, 0)
    small = pl.BlockSpec((1, HEAD_DIM), c2)
    rows = pl.BlockSpec((1, T, QK_WIDTH), lambda b: (b, 0, 0))
    stream = pl.BlockSpec((1, P, N_HEADS, HEAD_WIDTH), lambda b: (b, 0, 0, 0))
    return pl.pallas_call(
        functools.partial(_attn_sample_kernel, li=li),
        grid=(B,),
        in_specs=[small, small, small, small, pl.BlockSpec((1, HEAD_WIDTH), c2),
                  rows, stream, stream, rows, rows],
        out_specs=rows,
        out_shape=jax.ShapeDtypeStruct((B, T, QK_WIDTH), BF16),
        compiler_params=pltpu.CompilerParams(
            dimension_semantics=("parallel",), vmem_limit_bytes=VMEM_LIMIT),
        name="attn_sample",
    )(*lams, g_subln, qb, cache_k, cache_v, kb, vb)


def _lane_first_argmax(vals, lane):
    vmax = jnp.max(vals, axis=-1, keepdims=True)
    idx = jnp.min(jnp.where(vals == vmax, lane, LANES), axis=-1, keepdims=True)
    return vmax, idx


def _bf16_split(x):
    hi = x.astype(BF16)
    return hi, (x - hi.astype(F32)).astype(BF16)


def _route_sort_kernel(x_ref, pool_ref, attn_ref, wout_ref, gffn_ref, wr_hi_ref, wr_lo_ref, br_ref,
                       x1_ref, route_ref, seg_ref, hs_ref,
                       stage_ref, zero_ref, run_ref, sem_ref, *, tm, cap, ntiles):
    i = pl.program_id(0)
    slot = lax.rem(i, 2)
    slots = tm + N_GROUPS * SEG_ALIGN

    @pl.when(i == 0)
    def _():
        for g in range(N_GROUPS):
            run_ref[g] = 0
        stage_ref[...] = jnp.zeros(stage_ref.shape, BF16)
        zero_ref[...] = jnp.zeros(zero_ref.shape, BF16)

    mix = (jnp.dot(pool_ref[...], wout_ref[0:POOL_WIDTH, :], preferred_element_type=F32)
           + jnp.dot(attn_ref[...], wout_ref[POOL_WIDTH:, :], preferred_element_type=F32))
    x1 = x_ref[...] + mix
    x1_ref[...] = x1
    ms = jnp.mean(x1 * x1, axis=-1, keepdims=True)
    h2 = x1 * lax.rsqrt(ms + EPS) * gffn_ref[...]
    h_hi, h_lo = _bf16_split(h2)
    logits = (jnp.dot(h_hi, wr_hi_ref[...], preferred_element_type=F32)
              + jnp.dot(h_hi, wr_lo_ref[...], preferred_element_type=F32)
              + jnp.dot(h_lo, wr_hi_ref[...], preferred_element_type=F32)
              + br_ref[...])
    lane = lax.broadcasted_iota(jnp.int32, logits.shape, 1)
    neg = jnp.float32(-jnp.inf)
    gl = jnp.where(lane < N_GROUPS, logits, neg)
    gmax, g_idx = _lane_first_argmax(gl, lane)
    g_gate = 1.0 / jnp.sum(jnp.exp(gl - gmax), axis=-1, keepdims=True)
    e_lo = ROUTER_LANE0 + EXPERTS_PER_GROUP * g_idx
    el = jnp.where((lane >= e_lo) & (lane < e_lo + EXPERTS_PER_GROUP), logits, neg)
    v1, i1 = _lane_first_argmax(el, lane)
    v2, i2 = _lane_first_argmax(jnp.where(lane == i1, neg, el), lane)
    e21 = jnp.exp(v2 - v1)
    w1 = g_gate / (1.0 + e21)
    w2 = w1 * e21
    k1 = i1 - e_lo
    k2 = i2 - e_lo
    w1h = w1.astype(BF16).astype(F32)
    w2h = w2.astype(BF16).astype(F32)
    cparts = (jnp.where(lane == k1, w1h, 0.0) + jnp.where(lane == k2, w2h, 0.0)
              + jnp.where(lane == k1 + EXPERTS_PER_GROUP, w1 - w1h, 0.0)
              + jnp.where(lane == k2 + EXPERTS_PER_GROUP, w2 - w2h, 0.0))

    onehot = jnp.where(lane == g_idx, 1.0, 0.0)
    before = (lax.broadcasted_iota(jnp.int32, (tm, tm), 1)
              < lax.broadcasted_iota(jnp.int32, (tm, tm), 0))
    rank_all = jnp.dot(jnp.where(before, 1.0, 0.0).astype(BF16), onehot.astype(BF16),
                       preferred_element_type=F32)
    n = jnp.sum(onehot, axis=0, keepdims=True)
    n_pad = jnp.floor((n + (SEG_ALIGN - 1)) * (1.0 / SEG_ALIGN)) * SEG_ALIGN
    upper = (lax.broadcasted_iota(jnp.int32, (LANES, LANES), 0)
             < lax.broadcasted_iota(jnp.int32, (LANES, LANES), 1))
    off = jnp.dot(jnp.broadcast_to(n_pad, (8, LANES)).astype(BF16), jnp.where(upper, 1.0, 0.0).astype(BF16),
                  preferred_element_type=F32)[0:1]
    rank = jnp.sum(onehot * rank_all, axis=-1, keepdims=True)
    local = jnp.sum(onehot * off, axis=-1, keepdims=True) + rank
    route_ref[...] = jnp.broadcast_to(g_idx.astype(F32) * tm + rank, (tm, LANES))

    place = jnp.where(lax.broadcasted_iota(jnp.int32, (tm, slots), 1).astype(F32) == local, 1.0, 0.0)
    rows = jnp.concatenate([h_hi, cparts.astype(BF16)], axis=1)
    sorted_rows = lax.dot_general(place.astype(BF16), rows, (((0,), (0,)), ((), ())),
                                  preferred_element_type=F32)
    stage_ref[slot, 0:slots, :] = sorted_rows.astype(BF16)

    def window(g, slot_, src_row, dst_row):
        return pltpu.make_async_copy(
            stage_ref.at[slot_, pl.ds(pl.multiple_of(src_row, SEG_ALIGN), tm)],
            hs_ref.at[pl.ds(pl.multiple_of(dst_row, SEG_ALIGN), tm)],
            sem_ref.at[slot_])

    @pl.when(i > 0)
    def _():
        for g in range(N_GROUPS):
            window(g, 1 - slot, 0, 0).wait()

    for g in range(N_GROUPS):
        src = off[0, g].astype(jnp.int32)
        cnt = n_pad[0, g].astype(jnp.int32)
        start = run_ref[g]
        seg_ref[g, i] = start
        window(g, slot, src, g * cap + start).start()
        run_ref[g] = start + cnt

    @pl.when(i == ntiles - 1)
    def _():
        for g in range(N_GROUPS):
            window(g, slot, 0, 0).wait()
        fills = []
        for g in range(N_GROUPS):
            total = run_ref[g]
            seg_ref[g, ntiles] = total
            for z in range(ZERO_FILL_ROWS // ZERO_ROWS):
                dst = g * cap + total + z * ZERO_ROWS
                fills.append(pltpu.make_async_copy(
                    zero_ref, hs_ref.at[pl.ds(pl.multiple_of(dst, SEG_ALIGN), ZERO_ROWS)],
                    sem_ref.at[2]))
        for f in fills:
            f.start()
        for f in fills:
            f.wait()


def _round_up(x, m):
    return (x + m - 1) // m * m


def _sorted_capacity(n_tokens, ntiles):
    return _round_up(n_tokens + SEG_ALIGN * ntiles, MOE_BLOCK) + 3 * MOE_BLOCK


def _route_sort(x2d, pool2d, attn2d, w_out, g_ffn, wr_hi, wr_lo, br, *, tm):
    N = x2d.shape[0]
    ntiles = N // tm
    cap = _sorted_capacity(N, ntiles)
    row = lambda i: (i, 0)
    c2 = lambda i: (0, 0)
    return pl.pallas_call(
        functools.partial(_route_sort_kernel, tm=tm, cap=cap, ntiles=ntiles),
        grid=(ntiles,),
        in_specs=[pl.BlockSpec((tm, D_MODEL), row),
                  pl.BlockSpec((tm, POOL_WIDTH), row),
                  pl.BlockSpec((tm, QK_WIDTH), row),
                  pl.BlockSpec((D_MODEL, D_MODEL), c2),
                  pl.BlockSpec((1, D_MODEL), c2),
                  pl.BlockSpec((D_MODEL, LANES), c2),
                  pl.BlockSpec((D_MODEL, LANES), c2),
                  pl.BlockSpec((1, LANES), c2)],
        out_specs=[pl.BlockSpec((tm, D_MODEL), row),
                   pl.BlockSpec((tm, LANES), row),
                   pl.BlockSpec(memory_space=pltpu.SMEM),
                   pl.BlockSpec(memory_space=pl.ANY)],
        out_shape=[jax.ShapeDtypeStruct((N, D_MODEL), F32),
                   jax.ShapeDtypeStruct((N, LANES), F32),
                   jax.ShapeDtypeStruct((N_GROUPS, ntiles + 1), jnp.int32),
                   jax.ShapeDtypeStruct((N_GROUPS * cap, ROW_W), BF16)],
        scratch_shapes=[pltpu.VMEM((2, 2 * tm + N_GROUPS * SEG_ALIGN, ROW_W), BF16),
                        pltpu.VMEM((ZERO_ROWS, ROW_W), BF16),
                        pltpu.SMEM((N_GROUPS,), jnp.int32),
                        pltpu.SemaphoreType.DMA((3,))],
        compiler_params=pltpu.CompilerParams(
            dimension_semantics=("arbitrary",), vmem_limit_bytes=VMEM_LIMIT),
        name="route_sort",
    )(x2d, pool2d, attn2d, w_out, g_ffn, wr_hi, wr_lo, br)


def _group_moe_kernel(blk_ref, grp_ref, valid_ref, hs_ref, wg_ref, wu_ref, wd_ref, out_ref):
    b = pl.program_id(0)

    @pl.when(valid_ref[b] == 1)
    def _():
        rows = hs_ref[...]
        x = rows[:, :D_MODEL]
        cparts = rows[:, D_MODEL:].astype(F32)
        lane = lax.broadcasted_iota(jnp.int32, cparts.shape, 1)
        hes = []
        for e in range(EXPERTS_PER_GROUP):
            a = jnp.dot(x, wg_ref[e], preferred_element_type=F32)
            u = jnp.dot(x, wu_ref[e], preferred_element_type=F32)
            c = jnp.sum(jnp.where((lane == e) | (lane == e + EXPERTS_PER_GROUP), cparts, 0.0),
                        axis=-1, keepdims=True)
            hes.append((a * jax.nn.sigmoid(a) * u * c).astype(BF16))
        he = jnp.concatenate(hes, axis=1)
        out_ref[...] = jnp.dot(he, wd_ref[0], preferred_element_type=F32).astype(BF16)

    @pl.when(valid_ref[b] == 0)
    def _():
        out_ref[...] = jnp.zeros(out_ref.shape, BF16)


def _group_moe(blk, grp, valid, hs, w_gate, w_up, w_down_grouped):
    nb = blk.shape[0]
    bm = MOE_BLOCK
    by_block = lambda b, blk, grp, valid: (blk[b], 0)
    by_group = lambda b, blk, grp, valid: (grp[b], 0, 0)
    grid_spec = pltpu.PrefetchScalarGridSpec(
        num_scalar_prefetch=3,
        grid=(nb,),
        in_specs=[pl.BlockSpec((bm, ROW_W), by_block),
                  pl.BlockSpec((EXPERTS_PER_GROUP, D_MODEL, D_EXPERT), by_group),
                  pl.BlockSpec((EXPERTS_PER_GROUP, D_MODEL, D_EXPERT), by_group),
                  pl.BlockSpec((1, EXPERTS_PER_GROUP * D_EXPERT, D_MODEL), by_group)],
        out_specs=pl.BlockSpec((bm, D_MODEL), by_block),
    )
    return pl.pallas_call(
        _group_moe_kernel,
        grid_spec=grid_spec,
        out_shape=jax.ShapeDtypeStruct((hs.shape[0], D_MODEL), BF16),
        compiler_params=pltpu.CompilerParams(
            dimension_semantics=("arbitrary",), vmem_limit_bytes=VMEM_LIMIT),
        name="group_moe",
    )(blk, grp, valid, hs, w_gate, w_up, w_down_grouped)


def _block_table(seg, n_tokens, ntiles, cap):
    bm = MOE_BLOCK
    nb_max = _round_up(n_tokens + N_GROUPS * SEG_ALIGN * ntiles, bm) // bm + 2 * N_GROUPS
    total = seg[:, ntiles]
    nb = (total + bm - 1) // bm + 1
    end = jnp.cumsum(nb)
    start = end - nb
    b = jnp.arange(nb_max, dtype=jnp.int32)
    g = jnp.sum((b[:, None] >= end[None, :]).astype(jnp.int32), axis=1)
    valid = g < N_GROUPS
    gc = jnp.minimum(g, N_GROUPS - 1)
    blk = jnp.where(valid, gc * (cap // bm) + b - start[gc], N_GROUPS * cap // bm - 1)
    return blk.astype(jnp.int32), gc.astype(jnp.int32), valid.astype(jnp.int32)


def _unsort_norm_kernel(seg_ref, x1_ref, route_ref, gfin_ref, os_ref, y_ref, win_ref, sem_ref,
                        *, tm, cap, ntiles):
    i = pl.program_id(0)
    slot = lax.rem(i, 2)

    def window(tile, g, slot_):
        start = seg_ref[g * (ntiles + 1) + tile]
        return pltpu.make_async_copy(
            os_ref.at[pl.ds(pl.multiple_of(g * cap + start, SEG_ALIGN), tm)],
            win_ref.at[slot_, g], sem_ref.at[slot_])

    @pl.when(i == 0)
    def _():
        for g in range(N_GROUPS):
            window(0, g, 0).start()

    @pl.when(i + 1 < ntiles)
    def _():
        for g in range(N_GROUPS):
            window(i + 1, g, 1 - slot).start()

    for g in range(N_GROUPS):
        window(i, g, slot).wait()

    where = route_ref[:, 0:1]
    pick = jnp.where(lax.broadcasted_iota(jnp.int32, (tm, N_GROUPS * tm), 1).astype(F32) == where, 1.0, 0.0)
    moe = jnp.dot(pick.astype(BF16), win_ref[slot].reshape(N_GROUPS * tm, D_MODEL),
                  preferred_element_type=F32)
    x2 = x1_ref[...] + moe
    ms = jnp.mean(x2 * x2, axis=-1, keepdims=True)
    y_ref[...] = x2 * lax.rsqrt(ms + EPS) * gfin_ref[...]


def _unsort_norm(seg, x1, route, g_final, out_sorted, *, tm, cap):
    N = x1.shape[0]
    ntiles = N // tm
    grid_spec = pltpu.PrefetchScalarGridSpec(
        num_scalar_prefetch=1,
        grid=(ntiles,),
        in_specs=[pl.BlockSpec((tm, D_MODEL), lambda i, seg: (i, 0)),
                  pl.BlockSpec((tm, LANES), lambda i, seg: (i, 0)),
                  pl.BlockSpec((1, D_MODEL), lambda i, seg: (0, 0)),
                  pl.BlockSpec(memory_space=pl.ANY)],
        out_specs=pl.BlockSpec((tm, D_MODEL), lambda i, seg: (i, 0)),
        scratch_shapes=[pltpu.VMEM((2, N_GROUPS, tm, D_MODEL), BF16),
                        pltpu.SemaphoreType.DMA((2,))],
    )
    return pl.pallas_call(
        functools.partial(_unsort_norm_kernel, tm=tm, cap=cap, ntiles=ntiles),
        grid_spec=grid_spec,
        out_shape=jax.ShapeDtypeStruct((N, D_MODEL), F32),
        compiler_params=pltpu.CompilerParams(
            dimension_semantics=("arbitrary",), vmem_limit_bytes=VMEM_LIMIT),
        name="unsort_norm",
    )(seg.reshape(-1), x1, route, g_final, out_sorted)


def _layer(x, hist16, cache, start_pos, layer, w, g_final, *, tt, tm):
    B, T, _ = x.shape
    li = _lambda_init(layer)
    k, v, qb, kb, vb, pool, state = _inproj(x, hist16, w["g_mix"], w["w_in"], w["w_vt"], w["w_pool"],
                                            w["pool_scale"], tt=tt, start_pos=start_pos,
                                            transposed_v=cache is None)
    if cache is None:
        attn = _attn_prompt(w["lams"], w["g_subln"], qb, kb, vb, tq=256, li=li)
    else:
        attn = _attn_sample(w["lams"], w["g_subln"], qb, cache[0], cache[1], kb, vb, li=li)
    N = B * T
    ntiles = N // tm
    cap = _sorted_capacity(N, ntiles)
    x1, route, seg, hs = _route_sort(x.reshape(N, D_MODEL), pool.reshape(N, POOL_WIDTH),
                                     attn.reshape(N, QK_WIDTH), w["w_out"], w["g_ffn"],
                                     w["wr_hi"], w["wr_lo"], w["br"], tm=tm)
    blk, grp, valid = _block_table(seg, N, ntiles, cap)
    out_sorted = _group_moe(blk, grp, valid, hs, w["w_gate"], w["w_up"], w["w_down"])
    y = _unsort_norm(seg, x1, route, g_final, out_sorted, tm=tm, cap=cap)
    return y.reshape(B, T, D_MODEL), k, v, state


def kernel(x_prompt, x_sample, cache_k, cache_v, state_pool, g_mix, w_in, w_pool, pool_scale, lam_q1, lam_k1, lam_q2, lam_k2, g_subln, w_out, g_ffn, w_group, b_group, w_erouter, b_erouter, w_gate, w_up, w_down, g_final):
    depth = g_mix.shape[0]
    assert depth == 1, "the final norm is fused into the layer, so exactly one layer is supported"
    l = 0
    wr = jnp.concatenate([w_group[l], w_erouter[l]], axis=1)
    wr = jnp.pad(wr, ((0, 0), (0, LANES - wr.shape[1])))
    wr_hi = wr.astype(BF16)
    wr_lo = (wr - wr_hi.astype(F32)).astype(BF16)
    br = jnp.pad(jnp.concatenate([b_group[l], b_erouter[l]]), (0, LANES - N_GROUPS - N_EXPERTS))
    w = dict(
        g_mix=g_mix[l][None], w_in=w_in[l].astype(BF16),
        w_vt=w_in[l][:, POOL_WIDTH + 2 * QK_WIDTH:].T.astype(BF16), w_pool=w_pool[l].astype(BF16),
        pool_scale=pool_scale[l][None],
        lams=(lam_q1[l][None], lam_k1[l][None], lam_q2[l][None], lam_k2[l][None]),
        g_subln=g_subln[l][None], w_out=w_out[l].astype(BF16), g_ffn=g_ffn[l][None],
        wr_hi=wr_hi, wr_lo=wr_lo, br=br[None],
        w_gate=w_gate[l].astype(BF16), w_up=w_up[l].astype(BF16),
        w_down=w_down[l].astype(BF16).reshape(N_GROUPS, EXPERTS_PER_GROUP * D_EXPERT, D_MODEL),
    )
    gfin = g_final[None]
    Bp = x_prompt.shape[0]
    Bs, _, _ = x_sample.shape
    P = cache_k.shape[2]
    zero_hist = jnp.zeros((Bp, HIST_ROWS, POOL_WIDTH), F32)
    samp_hist = jnp.pad(state_pool[l], ((0, 0), (HIST_ROWS - POOL_HIST, 0), (0, 0)))
    cache = (cache_k[l], cache_v[l])

    yp, kp, vp, pp = _layer(x_prompt, zero_hist, None, 0, l, w, gfin, tt=512, tm=ROUTE_TILE)
    ys, kn, vn, pn = _layer(x_sample, samp_hist, cache, P, l, w, gfin, tt=64, tm=ROUTE_TILE)
    return (yp, ys, kp[None], vp[None], pp[None], kn[None], vn[None], pn[None])
```

```python
import functools
import math

import jax
import jax.numpy as jnp
from jax import lax
from jax.experimental import pallas as pl
from jax.experimental.pallas import tpu as pltpu

F32 = jnp.float32
BF16 = jnp.bfloat16

D_MODEL = 1024
CHUNK = 64
POOL_WIDTH = 512
POOL_WINDOWS = (2, 4, 8, 16)
POOL_GROUP = 128
POOL_HIST = 15
HIST_ROWS = 16
N_HEADS = 4
HEAD_DIM = 64
HEAD_WIDTH = 128
QK_WIDTH = 512
PROJ_WIDTH = 2048
ATTN_SCALE = HEAD_DIM ** -0.5
LOG2_E = math.log2(math.e)
NEG_INF = -1e30
N_GROUPS = 4
EXPERTS_PER_GROUP = 4
N_EXPERTS = 16
D_EXPERT = 256
EPS = 1e-6
LANES = 128
KV_BLOCK = 256
Q_SUB = 128
ATTN_HEADS_PER_STEP = 2
ROUTER_LANE0 = N_GROUPS
ROUTE_TILE = 256
SEG_ALIGN = 16
ROW_W = D_MODEL + LANES
MOE_BLOCK = 512
ZERO_ROWS = 256
ZERO_FILL_ROWS = 2 * MOE_BLOCK
VMEM_LIMIT = 48 * 1024 * 1024


def _lambda_init(layer):
    return 0.8 - 0.6 * math.exp(-0.3 * layer)


def _inproj_kernel(x_ref, hist_ref, gmix_ref, win_ref, wvt_ref, wpool_ref, pscale_ref,
                   k_ref, v_ref, qb_ref, kb_ref, vb_ref, pool_ref, state_ref,
                   ext_ref, *, tt, start_pos, transposed_v):
    t = pl.program_id(1)
    x = x_ref[0]
    ms = jnp.mean(x * x, axis=-1, keepdims=True)
    h = (x * lax.rsqrt(ms + EPS) * gmix_ref[...]).astype(BF16)
    proj = jnp.dot(h, win_ref[...], preferred_element_type=F32)
    u = proj[:, :POOL_WIDTH]
    q = proj[:, POOL_WIDTH:POOL_WIDTH + QK_WIDTH]
    k = proj[:, POOL_WIDTH + QK_WIDTH:POOL_WIDTH + 2 * QK_WIDTH]
    v = proj[:, POOL_WIDTH + 2 * QK_WIDTH:]
    for hd in range(N_HEADS):
        k_ref[0, :, hd, :] = k[:, hd * HEAD_WIDTH:(hd + 1) * HEAD_WIDTH]
        v_ref[0, :, hd, :] = v[:, hd * HEAD_WIDTH:(hd + 1) * HEAD_WIDTH]
    qb_ref[0] = (q * (ATTN_SCALE * LOG2_E if transposed_v else ATTN_SCALE)).astype(BF16)
    kb_ref[0] = k.astype(BF16)
    if transposed_v:
        vt = lax.dot_general(wvt_ref[...], h, (((1,), (1,)), ((), ())),
                             preferred_element_type=F32).astype(BF16)
        for i in range(tt // KV_BLOCK):
            vb_ref[0, i] = vt[:, i * KV_BLOCK:(i + 1) * KV_BLOCK]
    else:
        vb_ref[0] = v.astype(BF16)

    @pl.when(t == 0)
    def _():
        ext_ref[0:HIST_ROWS] = hist_ref[0]

    @pl.when(t > 0)
    def _():
        ext_ref[0:HIST_ROWS] = ext_ref[tt:tt + HIST_ROWS]

    ext_ref[HIST_ROWS:HIST_ROWS + tt] = u

    pos = start_pos + t * tt + lax.broadcasted_iota(jnp.int32, (tt, 1), 0)
    outs = []
    for g, w in enumerate(POOL_WINDOWS):
        lo = g * POOL_GROUP
        ug = u[:, lo:lo + POOL_GROUP]
        acc = ug
        for j in range(1, w):
            acc = acc + ext_ref[HIST_ROWS - j:HIST_ROWS - j + tt, lo:lo + POOL_GROUP]
        inv_cnt = 1.0 / jnp.minimum(w, pos + 1).astype(F32)
        d = acc * inv_cnt - ug
        outs.append(jnp.dot(d.astype(BF16), wpool_ref[g], preferred_element_type=F32))
    y = jnp.concatenate(outs, axis=-1) * pscale_ref[...]
    pool_ref[0] = y.astype(BF16)

    @pl.when(t == pl.num_programs(1) - 1)
    def _():
        state_ref[0] = ext_ref[tt + 1:tt + HIST_ROWS]


def _inproj(x, hist16, g_mix, w_in, w_vt, w_pool, pool_scale, *, tt, start_pos, transposed_v):
    B, T, _ = x.shape
    nt = T // tt
    row = lambda b, t: (b, t, 0)
    const2 = lambda b, t: (0, 0)
    wide = lambda dt: jax.ShapeDtypeStruct((B, T, QK_WIDTH), dt)
    if transposed_v:
        vb_spec = pl.BlockSpec((1, tt // KV_BLOCK, QK_WIDTH, KV_BLOCK), lambda b, t: (b, t, 0, 0))
        vb_shape = jax.ShapeDtypeStruct((B, T // KV_BLOCK, QK_WIDTH, KV_BLOCK), BF16)
    else:
        vb_spec = pl.BlockSpec((1, tt, QK_WIDTH), row)
        vb_shape = wide(BF16)
    return pl.pallas_call(
        functools.partial(_inproj_kernel, tt=tt, start_pos=start_pos, transposed_v=transposed_v),
        grid=(B, nt),
        in_specs=[
            pl.BlockSpec((1, tt, D_MODEL), row),
            pl.BlockSpec((1, HIST_ROWS, POOL_WIDTH), lambda b, t: (b, 0, 0)),
            pl.BlockSpec((1, D_MODEL), const2),
            pl.BlockSpec((D_MODEL, PROJ_WIDTH), const2),
            pl.BlockSpec((QK_WIDTH, D_MODEL), const2),
            pl.BlockSpec((len(POOL_WINDOWS), POOL_GROUP, POOL_GROUP), lambda b, t: (0, 0, 0)),
            pl.BlockSpec((1, POOL_WIDTH), const2),
        ],
        out_specs=[
            pl.BlockSpec((1, tt, N_HEADS, HEAD_WIDTH), lambda b, t: (b, t, 0, 0)),
            pl.BlockSpec((1, tt, N_HEADS, HEAD_WIDTH), lambda b, t: (b, t, 0, 0)),
            pl.BlockSpec((1, tt, QK_WIDTH), row),
            pl.BlockSpec((1, tt, QK_WIDTH), row),
            vb_spec,
            pl.BlockSpec((1, tt, POOL_WIDTH), row),
            pl.BlockSpec((1, POOL_HIST, POOL_WIDTH), lambda b, t: (b, 0, 0)),
        ],
        out_shape=[jax.ShapeDtypeStruct((B, T, N_HEADS, HEAD_WIDTH), F32),
                   jax.ShapeDtypeStruct((B, T, N_HEADS, HEAD_WIDTH), F32),
                   wide(BF16), wide(BF16), vb_shape, wide(BF16),
                   jax.ShapeDtypeStruct((B, POOL_HIST, POOL_WIDTH), F32)],
        scratch_shapes=[pltpu.VMEM((HIST_ROWS + tt, POOL_WIDTH), F32)],
        compiler_params=pltpu.CompilerParams(
            dimension_semantics=("parallel", "arbitrary"), vmem_limit_bytes=VMEM_LIMIT),
        name="inproj_pool",
    )(x, hist16, g_mix, w_in, w_vt, w_pool, pool_scale)


def _lambda_value(lq1_ref, lk1_ref, lq2_ref, lk2_ref, li):
    s1 = jnp.sum(lq1_ref[...] * lk1_ref[...], axis=-1, keepdims=True)
    s2 = jnp.sum(lq2_ref[...] * lk2_ref[...], axis=-1, keepdims=True)
    return jnp.exp(s1) - jnp.exp(s2) + li


def _split_halves(q):
    lane = lax.broadcasted_iota(jnp.int32, (1, HEAD_WIDTH), 1)
    zero = jnp.zeros_like(q)
    return jnp.where(lane < HEAD_DIM, q, zero), jnp.where(lane >= HEAD_DIM, q, zero)


def _scores(qh, kblk):
    return lax.dot_general(qh, kblk, (((1,), (1,)), ((), ())), preferred_element_type=F32)


def _subln(o, gsub_ref, li):
    return o * lax.rsqrt(jnp.mean(o * o, axis=-1, keepdims=True) + EPS) * gsub_ref[...] * (1.0 - li)


def _attn_prompt_kernel(lq1_ref, lk1_ref, lq2_ref, lk2_ref, gsub_ref, bias_ref, q_ref, k_ref, vt_ref, o_ref,
                        m_ref, l_ref, acc_ref, *, tq, heads, li):
    tk = KV_BLOCK
    n_sub = tq // Q_SUB
    n_tiles = q_ref.shape[1] // tq
    head_cols = [slice(hh * HEAD_WIDTH, (hh + 1) * HEAD_WIDTH) for hh in range(heads)]

    def query_maps(qi):
        q2 = []
        for cols in head_cols:
            for a in range(n_sub):
                rows = q_ref[0, pl.ds(pl.multiple_of(qi * tq + a * Q_SUB, Q_SUB), Q_SUB), cols]
                q0, q1 = _split_halves(rows)
                q2.append(jnp.concatenate([q0, q1], axis=0))
        return q2

    def qk(q2, j, masked):
        out = []
        for hh, cols in enumerate(head_cols):
            kblk = k_ref[0, pl.ds(pl.multiple_of(j * tk, tk), tk), cols]
            for a in range(n_sub):
                s = _scores(kblk, q2[hh * n_sub + a])
                if masked:
                    s = s + bias_ref[a]
                out.append((s, jnp.max(s, axis=0, keepdims=True)))
        return tuple(out)

    ones_rows = jnp.ones((SEG_ALIGN, tk), BF16)

    def consume(j, sc):
        for hh, cols in enumerate(head_cols):
            vtb = jnp.concatenate([vt_ref[0, j, cols, :], ones_rows], axis=0)
            for a in range(n_sub):
                c = hh * n_sub + a
                s, cmax = sc[c]
                m_old = m_ref[c]
                m_new = jnp.maximum(m_old, cmax)
                alpha = jnp.exp2(m_old - m_new)
                p = jnp.exp2(s - m_new)
                pv = jnp.dot(vtb, p.astype(BF16), preferred_element_type=F32)
                l_ref[c] = alpha * l_ref[c] + pv[HEAD_WIDTH:HEAD_WIDTH + 1]
                acc_ref[c] = alpha * acc_ref[c] + pv[:HEAD_WIDTH]
                m_ref[c] = m_new

    lam = _lambda_value(lq1_ref, lk1_ref, lq2_ref, lk2_ref, li)
    scale = gsub_ref[...] * (1.0 - li)

    def tile(qi, sc):
        q2 = query_maps(qi)
        m_ref[...] = jnp.full(m_ref.shape, NEG_INF, F32)
        l_ref[...] = jnp.zeros(l_ref.shape, F32)
        acc_ref[...] = jnp.zeros(acc_ref.shape, F32)

        def body(j, carry):
            prev, cur = carry
            nxt = qk(q2, j, False)
            consume(prev, cur)
            return j, nxt

        last, cur = lax.fori_loop(0, qi, body, (qi, sc))
        qn = jnp.minimum(qi + 1, n_tiles - 1)
        nxt = qk(query_maps(qn), qn, True)
        consume(last, cur)
        for hh, cols in enumerate(head_cols):
            for a in range(n_sub):
                c = hh * n_sub + a
                o2 = acc_ref[c] / l_ref[c]
                o_t = o2[:, :Q_SUB] - lam * o2[:, Q_SUB:]
                o_t = o_t * lax.rsqrt(jnp.mean(o_t * o_t, axis=0, keepdims=True) + EPS)
                rows = pl.ds(pl.multiple_of(qi * tq + a * Q_SUB, Q_SUB), Q_SUB)
                o_ref[0, rows, cols] = (o_t.T * scale).astype(BF16)
        return nxt

    lax.fori_loop(0, n_tiles, tile, qk(query_maps(0), 0, True))


def _attn_prompt(lams, g_subln, qb, kb, vt, *, tq, heads, li):
    B, T, _ = qb.shape
    assert tq == KV_BLOCK and tq % Q_SUB == 0
    key_chunk = jnp.arange(KV_BLOCK)[None, :, None] // CHUNK
    query = jnp.arange(tq // Q_SUB)[:, None, None] * Q_SUB + jnp.arange(2 * Q_SUB)[None, None, :] % Q_SUB
    bias = jnp.where(key_chunk <= query // CHUNK, 0.0, NEG_INF).astype(F32)
    c2 = lambda b, h: (0, 0)
    small = pl.BlockSpec((1, HEAD_DIM), c2)
    width = heads * HEAD_WIDTH
    chains = heads * (tq // Q_SUB)
    head = pl.BlockSpec((1, T, width), lambda b, h: (b, 0, h))
    return pl.pallas_call(
        functools.partial(_attn_prompt_kernel, tq=tq, heads=heads, li=li),
        grid=(B, N_HEADS // heads),
        in_specs=[small, small, small, small,
                  pl.BlockSpec((1, HEAD_WIDTH), c2),
                  pl.BlockSpec(bias.shape, lambda b, h: (0, 0, 0)),
                  head, head,
                  pl.BlockSpec((1, T // KV_BLOCK, width, KV_BLOCK), lambda b, h: (b, 0, h, 0))],
        out_specs=head,
        out_shape=jax.ShapeDtypeStruct((B, T, QK_WIDTH), BF16),
        scratch_shapes=[pltpu.VMEM((chains, 1, 2 * Q_SUB), F32),
                        pltpu.VMEM((chains, 1, 2 * Q_SUB), F32),
                        pltpu.VMEM((chains, HEAD_WIDTH, 2 * Q_SUB), F32)],
        compiler_params=pltpu.CompilerParams(
            dimension_semantics=("parallel", "parallel"), vmem_limit_bytes=VMEM_LIMIT),
        name="attn_prompt",
    )(*lams, g_subln, bias, qb, kb, vt)


def _attn_sample_kernel(lq1_ref, lk1_ref, lq2_ref, lk2_ref, gsub_ref, q_ref, kc_ref, vc_ref,
                        kn_ref, vn_ref, o_ref, *, li):
    T = q_ref.shape[1]
    lam = _lambda_value(lq1_ref, lk1_ref, lq2_ref, lk2_ref, li)
    for hd in range(N_HEADS):
        cols = slice(hd * HEAD_WIDTH, (hd + 1) * HEAD_WIDTH)
        q2 = jnp.concatenate(_split_halves(q_ref[0, :, cols]), axis=0)
        kc = kc_ref[0, :, hd, :].astype(BF16)
        vc = vc_ref[0, :, hd, :].astype(BF16)
        sc = _scores(q2, kc)
        sn = _scores(q2, kn_ref[0, :, cols])
        m = jnp.maximum(jnp.max(sc, axis=-1, keepdims=True), jnp.max(sn, axis=-1, keepdims=True))
        pc = jnp.exp(sc - m)
        pn = jnp.exp(sn - m)
        l = jnp.sum(pc, axis=-1, keepdims=True) + jnp.sum(pn, axis=-1, keepdims=True)
        acc = (jnp.dot(pc.astype(BF16), vc, preferred_element_type=F32)
               + jnp.dot(pn.astype(BF16), vn_ref[0, :, cols], preferred_element_type=F32))
        o2 = acc / l
        o = o2[:T] - lam * o2[T:]
        o_ref[0, :, cols] = _subln(o, gsub_ref, li).astype(BF16)


def _attn_sample(lams, g_subln, qb, cache_k, cache_v, kb, vb, *, li):
    B, T, _ = qb.shape
    P = cache_k.shape[1]
    c2 = lambda b: (0, 0)
    small = pl.BlockSpec((1, HEAD_DIM), c2)
    rows = pl.BlockSpec((1, T, QK_WIDTH), lambda b: (b, 0, 0))
    stream = pl.BlockSpec((1, P, N_HEADS, HEAD_WIDTH), lambda b: (b, 0, 0, 0))
    return pl.pallas_call(
        functools.partial(_attn_sample_kernel, li=li),
        grid=(B,),
        in_specs=[small, small, small, small, pl.BlockSpec((1, HEAD_WIDTH), c2),
                  rows, stream, stream, rows, rows],
        out_specs=rows,
        out_shape=jax.ShapeDtypeStruct((B, T, QK_WIDTH), BF16),
        compiler_params=pltpu.CompilerParams(
            dimension_semantics=("parallel",), vmem_limit_bytes=VMEM_LIMIT),
        name="attn_sample",
    )(*lams, g_subln, qb, cache_k, cache_v, kb, vb)


def _lane_first_argmax(vals, lane):
    vmax = jnp.max(vals, axis=-1, keepdims=True)
    idx = jnp.min(jnp.where(vals == vmax, lane, LANES), axis=-1, keepdims=True)
    return vmax, idx


def _bf16_split(x):
    hi = x.astype(BF16)
    return hi, (x - hi.astype(F32)).astype(BF16)


def _route_sort_kernel(x_ref, pool_ref, attn_ref, wout_ref, gffn_ref, wr_hi_ref, wr_lo_ref, br_ref,
                       x1_ref, route_ref, seg_ref, hs_ref,
                       stage_ref, zero_ref, run_ref, sem_ref, *, tm, cap, ntiles):
    i = pl.program_id(0)
    slot = lax.rem(i, 2)
    slots = tm + N_GROUPS * SEG_ALIGN

    @pl.when(i == 0)
    def _():
        for g in range(N_GROUPS):
            run_ref[g] = 0
        stage_ref[...] = jnp.zeros(stage_ref.shape, BF16)
        zero_ref[...] = jnp.zeros(zero_ref.shape, BF16)

    mix = (jnp.dot(pool_ref[...], wout_ref[0:POOL_WIDTH, :], preferred_element_type=F32)
           + jnp.dot(attn_ref[...], wout_ref[POOL_WIDTH:, :], preferred_element_type=F32))
    x1 = x_ref[...] + mix
    x1_ref[...] = x1
    ms = jnp.mean(x1 * x1, axis=-1, keepdims=True)
    h2 = x1 * lax.rsqrt(ms + EPS) * gffn_ref[...]
    h_hi, h_lo = _bf16_split(h2)
    logits = (jnp.dot(h_hi, wr_hi_ref[...], preferred_element_type=F32)
              + jnp.dot(h_hi, wr_lo_ref[...], preferred_element_type=F32)
              + jnp.dot(h_lo, wr_hi_ref[...], preferred_element_type=F32)
              + br_ref[...])
    lane = lax.broadcasted_iota(jnp.int32, logits.shape, 1)
    neg = jnp.float32(-jnp.inf)
    gl = jnp.where(lane < N_GROUPS, logits, neg)
    gmax, g_idx = _lane_first_argmax(gl, lane)
    g_gate = 1.0 / jnp.sum(jnp.exp(gl - gmax), axis=-1, keepdims=True)
    e_lo = ROUTER_LANE0 + EXPERTS_PER_GROUP * g_idx
    el = jnp.where((lane >= e_lo) & (lane < e_lo + EXPERTS_PER_GROUP), logits, neg)
    v1, i1 = _lane_first_argmax(el, lane)
    v2, i2 = _lane_first_argmax(jnp.where(lane == i1, neg, el), lane)
    e21 = jnp.exp(v2 - v1)
    w1 = g_gate / (1.0 + e21)
    w2 = w1 * e21
    k1 = i1 - e_lo
    k2 = i2 - e_lo
    w1h = w1.astype(BF16).astype(F32)
    w2h = w2.astype(BF16).astype(F32)
    cparts = (jnp.where(lane == k1, w1h, 0.0) + jnp.where(lane == k2, w2h, 0.0)
              + jnp.where(lane == k1 + EXPERTS_PER_GROUP, w1 - w1h, 0.0)
              + jnp.where(lane == k2 + EXPERTS_PER_GROUP, w2 - w2h, 0.0))

    onehot = jnp.where(lane == g_idx, 1.0, 0.0)
    before = (lax.broadcasted_iota(jnp.int32, (tm, tm), 1)
              < lax.broadcasted_iota(jnp.int32, (tm, tm), 0))
    rank_all = jnp.dot(jnp.where(before, 1.0, 0.0).astype(BF16), onehot.astype(BF16),
                       preferred_element_type=F32)
    n = jnp.sum(onehot, axis=0, keepdims=True)
    n_pad = jnp.floor((n + (SEG_ALIGN - 1)) * (1.0 / SEG_ALIGN)) * SEG_ALIGN
    upper = (lax.broadcasted_iota(jnp.int32, (LANES, LANES), 0)
             < lax.broadcasted_iota(jnp.int32, (LANES, LANES), 1))
    off = jnp.dot(jnp.broadcast_to(n_pad, (8, LANES)).astype(BF16), jnp.where(upper, 1.0, 0.0).astype(BF16),
                  preferred_element_type=F32)[0:1]
    rank = jnp.sum(onehot * rank_all, axis=-1, keepdims=True)
    local = jnp.sum(onehot * off, axis=-1, keepdims=True) + rank
    route_ref[...] = jnp.broadcast_to(g_idx.astype(F32) * tm + rank, (tm, LANES))

    place = jnp.where(lax.broadcasted_iota(jnp.int32, (tm, slots), 1).astype(F32) == local, 1.0, 0.0)
    rows = jnp.concatenate([h_hi, cparts.astype(BF16)], axis=1)
    sorted_rows = lax.dot_general(place.astype(BF16), rows, (((0,), (0,)), ((), ())),
                                  preferred_element_type=F32)
    stage_ref[slot, 0:slots, :] = sorted_rows.astype(BF16)

    def window(g, slot_, src_row, dst_row):
        return pltpu.make_async_copy(
            stage_ref.at[slot_, pl.ds(pl.multiple_of(src_row, SEG_ALIGN), tm)],
            hs_ref.at[pl.ds(pl.multiple_of(dst_row, SEG_ALIGN), tm)],
            sem_ref.at[slot_])

    @pl.when(i > 0)
    def _():
        for g in range(N_GROUPS):
            window(g, 1 - slot, 0, 0).wait()

    for g in range(N_GROUPS):
        src = off[0, g].astype(jnp.int32)
        cnt = n_pad[0, g].astype(jnp.int32)
        start = run_ref[g]
        seg_ref[g, i] = start
        window(g, slot, src, g * cap + start).start()
        run_ref[g] = start + cnt

    @pl.when(i == ntiles - 1)
    def _():
        for g in range(N_GROUPS):
            window(g, slot, 0, 0).wait()
        fills = []
        for g in range(N_GROUPS):
            total = run_ref[g]
            seg_ref[g, ntiles] = total
            for z in range(ZERO_FILL_ROWS // ZERO_ROWS):
                dst = g * cap + total + z * ZERO_ROWS
                fills.append(pltpu.make_async_copy(
                    zero_ref, hs_ref.at[pl.ds(pl.multiple_of(dst, SEG_ALIGN), ZERO_ROWS)],
                    sem_ref.at[2]))
        for f in fills:
            f.start()
        for f in fills:
            f.wait()


def _round_up(x, m):
    return (x + m - 1) // m * m


def _sorted_capacity(n_tokens, ntiles):
    return _round_up(n_tokens + SEG_ALIGN * ntiles, MOE_BLOCK) + 3 * MOE_BLOCK


def _route_sort(x2d, pool2d, attn2d, w_out, g_ffn, wr_hi, wr_lo, br, *, tm):
    N = x2d.shape[0]
    ntiles = N // tm
    cap = _sorted_capacity(N, ntiles)
    row = lambda i: (i, 0)
    c2 = lambda i: (0, 0)
    return pl.pallas_call(
        functools.partial(_route_sort_kernel, tm=tm, cap=cap, ntiles=ntiles),
        grid=(ntiles,),
        in_specs=[pl.BlockSpec((tm, D_MODEL), row),
                  pl.BlockSpec((tm, POOL_WIDTH), row),
                  pl.BlockSpec((tm, QK_WIDTH), row),
                  pl.BlockSpec((D_MODEL, D_MODEL), c2),
                  pl.BlockSpec((1, D_MODEL), c2),
                  pl.BlockSpec((D_MODEL, LANES), c2),
                  pl.BlockSpec((D_MODEL, LANES), c2),
                  pl.BlockSpec((1, LANES), c2)],
        out_specs=[pl.BlockSpec((tm, D_MODEL), row),
                   pl.BlockSpec((tm, LANES), row),
                   pl.BlockSpec(memory_space=pltpu.SMEM),
                   pl.BlockSpec(memory_space=pl.ANY)],
        out_shape=[jax.ShapeDtypeStruct((N, D_MODEL), F32),
                   jax.ShapeDtypeStruct((N, LANES), F32),
                   jax.ShapeDtypeStruct((N_GROUPS, ntiles + 1), jnp.int32),
                   jax.ShapeDtypeStruct((N_GROUPS * cap, ROW_W), BF16)],
        scratch_shapes=[pltpu.VMEM((2, 2 * tm + N_GROUPS * SEG_ALIGN, ROW_W), BF16),
                        pltpu.VMEM((ZERO_ROWS, ROW_W), BF16),
                        pltpu.SMEM((N_GROUPS,), jnp.int32),
                        pltpu.SemaphoreType.DMA((3,))],
        compiler_params=pltpu.CompilerParams(
            dimension_semantics=("arbitrary",), vmem_limit_bytes=VMEM_LIMIT),
        name="route_sort",
    )(x2d, pool2d, attn2d, w_out, g_ffn, wr_hi, wr_lo, br)


def _group_moe_kernel(blk_ref, grp_ref, valid_ref, hs_ref, wg_ref, wu_ref, wd_ref, out_ref):
    b = pl.program_id(0)

    @pl.when(valid_ref[b] == 1)
    def _():
        rows = hs_ref[...]
        x = rows[:, :D_MODEL]
        cparts = rows[:, D_MODEL:].astype(F32)
        lane = lax.broadcasted_iota(jnp.int32, cparts.shape, 1)
        hes = []
        for e in range(EXPERTS_PER_GROUP):
            a = jnp.dot(x, wg_ref[e], preferred_element_type=F32)
            u = jnp.dot(x, wu_ref[e], preferred_element_type=F32)
            c = jnp.sum(jnp.where((lane == e) | (lane == e + EXPERTS_PER_GROUP), cparts, 0.0),
                        axis=-1, keepdims=True)
            hes.append((a * jax.nn.sigmoid(a) * u * c).astype(BF16))
        he = jnp.concatenate(hes, axis=1)
        out_ref[...] = jnp.dot(he, wd_ref[0], preferred_element_type=F32).astype(BF16)

    @pl.when(valid_ref[b] == 0)
    def _():
        out_ref[...] = jnp.zeros(out_ref.shape, BF16)


def _group_moe(blk, grp, valid, hs, w_gate, w_up, w_down_grouped):
    nb = blk.shape[0]
    bm = MOE_BLOCK
    by_block = lambda b, blk, grp, valid: (blk[b], 0)
    by_group = lambda b, blk, grp, valid: (grp[b], 0, 0)
    grid_spec = pltpu.PrefetchScalarGridSpec(
        num_scalar_prefetch=3,
        grid=(nb,),
        in_specs=[pl.BlockSpec((bm, ROW_W), by_block),
                  pl.BlockSpec((EXPERTS_PER_GROUP, D_MODEL, D_EXPERT), by_group),
                  pl.BlockSpec((EXPERTS_PER_GROUP, D_MODEL, D_EXPERT), by_group),
                  pl.BlockSpec((1, EXPERTS_PER_GROUP * D_EXPERT, D_MODEL), by_group)],
        out_specs=pl.BlockSpec((bm, D_MODEL), by_block),
    )
    return pl.pallas_call(
        _group_moe_kernel,
        grid_spec=grid_spec,
        out_shape=jax.ShapeDtypeStruct((hs.shape[0], D_MODEL), BF16),
        compiler_params=pltpu.CompilerParams(
            dimension_semantics=("arbitrary",), vmem_limit_bytes=VMEM_LIMIT),
        name="group_moe",
    )(blk, grp, valid, hs, w_gate, w_up, w_down_grouped)


def _block_table(seg, n_tokens, ntiles, cap):
    bm = MOE_BLOCK
    nb_max = _round_up(n_tokens + N_GROUPS * SEG_ALIGN * ntiles, bm) // bm + 2 * N_GROUPS
    total = seg[:, ntiles]
    nb = (total + bm - 1) // bm + 1
    end = jnp.cumsum(nb)
    start = end - nb
    b = jnp.arange(nb_max, dtype=jnp.int32)
    g = jnp.sum((b[:, None] >= end[None, :]).astype(jnp.int32), axis=1)
    valid = g < N_GROUPS
    gc = jnp.minimum(g, N_GROUPS - 1)
    blk = jnp.where(valid, gc * (cap // bm) + b - start[gc], N_GROUPS * cap // bm - 1)
    return blk.astype(jnp.int32), gc.astype(jnp.int32), valid.astype(jnp.int32)


def _unsort_norm_kernel(seg_ref, x1_ref, route_ref, gfin_ref, os_ref, y_ref, win_ref, sem_ref,
                        *, tm, cap, ntiles):
    i = pl.program_id(0)
    slot = lax.rem(i, 2)

    def window(tile, g, slot_):
        start = seg_ref[g * (ntiles + 1) + tile]
        return pltpu.make_async_copy(
            os_ref.at[pl.ds(pl.multiple_of(g * cap + start, SEG_ALIGN), tm)],
            win_ref.at[slot_, g], sem_ref.at[slot_])

    @pl.when(i == 0)
    def _():
        for g in range(N_GROUPS):
            window(0, g, 0).start()

    @pl.when(i + 1 < ntiles)
    def _():
        for g in range(N_GROUPS):
            window(i + 1, g, 1 - slot).start()

    for g in range(N_GROUPS):
        window(i, g, slot).wait()

    where = route_ref[:, 0:1]
    pick = jnp.where(lax.broadcasted_iota(jnp.int32, (tm, N_GROUPS * tm), 1).astype(F32) == where, 1.0, 0.0)
    moe = jnp.dot(pick.astype(BF16), win_ref[slot].reshape(N_GROUPS * tm, D_MODEL),
                  preferred_element_type=F32)
    x2 = x1_ref[...] + moe
    ms = jnp.mean(x2 * x2, axis=-1, keepdims=True)
    y_ref[...] = x2 * lax.rsqrt(ms + EPS) * gfin_ref[...]


def _unsort_norm(seg, x1, route, g_final, out_sorted, *, tm, cap):
    N = x1.shape[0]
    ntiles = N // tm
    grid_spec = pltpu.PrefetchScalarGridSpec(
        num_scalar_prefetch=1,
        grid=(ntiles,),
        in_specs=[pl.BlockSpec((tm, D_MODEL), lambda i, seg: (i, 0)),
                  pl.BlockSpec((tm, LANES), lambda i, seg: (i, 0)),
                  pl.BlockSpec((1, D_MODEL), lambda i, seg: (0, 0)),
                  pl.BlockSpec(memory_space=pl.ANY)],
        out_specs=pl.BlockSpec((tm, D_MODEL), lambda i, seg: (i, 0)),
        scratch_shapes=[pltpu.VMEM((2, N_GROUPS, tm, D_MODEL), BF16),
                        pltpu.SemaphoreType.DMA((2,))],
    )
    return pl.pallas_call(
        functools.partial(_unsort_norm_kernel, tm=tm, cap=cap, ntiles=ntiles),
        grid_spec=grid_spec,
        out_shape=jax.ShapeDtypeStruct((N, D_MODEL), F32),
        compiler_params=pltpu.CompilerParams(
            dimension_semantics=("arbitrary",), vmem_limit_bytes=VMEM_LIMIT),
        name="unsort_norm",
    )(seg.reshape(-1), x1, route, g_final, out_sorted)


def _layer(x, hist16, cache, start_pos, layer, w, g_final, *, tt, tm):
    B, T, _ = x.shape
    li = _lambda_init(layer)
    k, v, qb, kb, vb, pool, state = _inproj(x, hist16, w["g_mix"], w["w_in"], w["w_vt"], w["w_pool"],
                                            w["pool_scale"], tt=tt, start_pos=start_pos,
                                            transposed_v=cache is None)
    if cache is None:
        attn = _attn_prompt(w["lams"], w["g_subln"], qb, kb, vb, tq=KV_BLOCK, heads=ATTN_HEADS_PER_STEP, li=li)
    else:
        attn = _attn_sample(w["lams"], w["g_subln"], qb, cache[0], cache[1], kb, vb, li=li)
    N = B * T
    ntiles = N // tm
    cap = _sorted_capacity(N, ntiles)
    x1, route, seg, hs = _route_sort(x.reshape(N, D_MODEL), pool.reshape(N, POOL_WIDTH),
                                     attn.reshape(N, QK_WIDTH), w["w_out"], w["g_ffn"],
                                     w["wr_hi"], w["wr_lo"], w["br"], tm=tm)
    blk, grp, valid = _block_table(seg, N, ntiles, cap)
    out_sorted = _group_moe(blk, grp, valid, hs, w["w_gate"], w["w_up"], w["w_down"])
    y = _unsort_norm(seg, x1, route, g_final, out_sorted, tm=tm, cap=cap)
    return y.reshape(B, T, D_MODEL), k, v, state


def kernel(x_prompt, x_sample, cache_k, cache_v, state_pool, g_mix, w_in, w_pool, pool_scale, lam_q1, lam_k1, lam_q2, lam_k2, g_subln, w_out, g_ffn, w_group, b_group, w_erouter, b_erouter, w_gate, w_up, w_down, g_final):
    depth = g_mix.shape[0]
    assert depth == 1, "the final norm is fused into the layer, so exactly one layer is supported"
    l = 0
    wr = jnp.concatenate([w_group[l], w_erouter[l]], axis=1)
    wr = jnp.pad(wr, ((0, 0), (0, LANES - wr.shape[1])))
    wr_hi = wr.astype(BF16)
    wr_lo = (wr - wr_hi.astype(F32)).astype(BF16)
    br = jnp.pad(jnp.concatenate([b_group[l], b_erouter[l]]), (0, LANES - N_GROUPS - N_EXPERTS))
    w = dict(
        g_mix=g_mix[l][None], w_in=w_in[l].astype(BF16),
        w_vt=w_in[l][:, POOL_WIDTH + 2 * QK_WIDTH:].T.astype(BF16), w_pool=w_pool[l].astype(BF16),
        pool_scale=pool_scale[l][None],
        lams=(lam_q1[l][None], lam_k1[l][None], lam_q2[l][None], lam_k2[l][None]),
        g_subln=g_subln[l][None], w_out=w_out[l].astype(BF16), g_ffn=g_ffn[l][None],
        wr_hi=wr_hi, wr_lo=wr_lo, br=br[None],
        w_gate=w_gate[l].astype(BF16), w_up=w_up[l].astype(BF16),
        w_down=w_down[l].astype(BF16).reshape(N_GROUPS, EXPERTS_PER_GROUP * D_EXPERT, D_MODEL),
    )
    gfin = g_final[None]
    Bp = x_prompt.shape[0]
    Bs, _, _ = x_sample.shape
    P = cache_k.shape[2]
    zero_hist = jnp.zeros((Bp, HIST_ROWS, POOL_WIDTH), F32)
    samp_hist = jnp.pad(state_pool[l], ((0, 0), (HIST_ROWS - POOL_HIST, 0), (0, 0)))
    cache = (cache_k[l], cache_v[l])

    yp, kp, vp, pp = _layer(x_prompt, zero_hist, None, 0, l, w, gfin, tt=512, tm=ROUTE_TILE)
    ys, kn, vn, pn = _layer(x_sample, samp_hist, cache, P, l, w, gfin, tt=64, tm=ROUTE_TILE)
    return (yp, ys, kp[None], vp[None], pp[None], kn[None], vn[None], pn[None])
```

```python
import functools
import math

import jax
import jax.numpy as jnp
from jax import lax
from jax.experimental import pallas as pl
from jax.experimental.pallas import tpu as pltpu

F32 = jnp.float32
BF16 = jnp.bfloat16

D_MODEL = 1024
CHUNK = 64
POOL_WIDTH = 512
POOL_WINDOWS = (2, 4, 8, 16)
POOL_GROUP = 128
POOL_HIST = 15
HIST_ROWS = 16
N_HEADS = 4
HEAD_DIM = 64
HEAD_WIDTH = 128
QK_WIDTH = 512
PROJ_WIDTH = 2048
ATTN_SCALE = HEAD_DIM ** -0.5
LOG2_E = math.log2(math.e)
NEG_INF = -1e30
N_GROUPS = 4
EXPERTS_PER_GROUP = 4
N_EXPERTS = 16
D_EXPERT = 256
EPS = 1e-6
LANES = 128
KV_BLOCK = 256
Q_SUB = 128
ATTN_HEADS_PER_STEP = 2
ROUTER_LANE0 = N_GROUPS
ROUTE_TILE = 256
ROUTE_STEP = 512
ROUTE_LANE = 8
SEG_ALIGN = 16
ROW_W = D_MODEL + LANES
MOE_BLOCK = 512
ZERO_ROWS = 256
ZERO_FILL_ROWS = 2 * MOE_BLOCK
VMEM_LIMIT = 48 * 1024 * 1024


def _lambda_init(layer):
    return 0.8 - 0.6 * math.exp(-0.3 * layer)


def _inproj_kernel(x_ref, hist_ref, gmix_ref, win_ref, wvt_ref, wpool_ref, pscale_ref,
                   k_ref, v_ref, qb_ref, kb_ref, vb_ref, pool_ref, state_ref,
                   ext_ref, *, tt, start_pos, transposed_v):
    t = pl.program_id(1)
    x = x_ref[0]
    ms = jnp.mean(x * x, axis=-1, keepdims=True)
    h = (x * lax.rsqrt(ms + EPS) * gmix_ref[...]).astype(BF16)
    proj = jnp.dot(h, win_ref[...], preferred_element_type=F32)
    u = proj[:, :POOL_WIDTH]
    q = proj[:, POOL_WIDTH:POOL_WIDTH + QK_WIDTH]
    k = proj[:, POOL_WIDTH + QK_WIDTH:POOL_WIDTH + 2 * QK_WIDTH]
    v = proj[:, POOL_WIDTH + 2 * QK_WIDTH:]
    for hd in range(N_HEADS):
        k_ref[0, :, hd, :] = k[:, hd * HEAD_WIDTH:(hd + 1) * HEAD_WIDTH]
        v_ref[0, :, hd, :] = v[:, hd * HEAD_WIDTH:(hd + 1) * HEAD_WIDTH]
    qb_ref[0] = (q * (ATTN_SCALE * LOG2_E if transposed_v else ATTN_SCALE)).astype(BF16)
    kb_ref[0] = k.astype(BF16)
    if transposed_v:
        vt = lax.dot_general(wvt_ref[...], h, (((1,), (1,)), ((), ())),
                             preferred_element_type=F32).astype(BF16)
        for i in range(tt // KV_BLOCK):
            vb_ref[0, i] = vt[:, i * KV_BLOCK:(i + 1) * KV_BLOCK]
    else:
        vb_ref[0] = v.astype(BF16)

    @pl.when(t == 0)
    def _():
        ext_ref[0:HIST_ROWS] = hist_ref[0]

    @pl.when(t > 0)
    def _():
        ext_ref[0:HIST_ROWS] = ext_ref[tt:tt + HIST_ROWS]

    ext_ref[HIST_ROWS:HIST_ROWS + tt] = u

    pos = start_pos + t * tt + lax.broadcasted_iota(jnp.int32, (tt, 1), 0)
    outs = []
    for g, w in enumerate(POOL_WINDOWS):
        lo = g * POOL_GROUP
        ug = u[:, lo:lo + POOL_GROUP]
        acc = ug
        for j in range(1, w):
            acc = acc + ext_ref[HIST_ROWS - j:HIST_ROWS - j + tt, lo:lo + POOL_GROUP]
        inv_cnt = 1.0 / jnp.minimum(w, pos + 1).astype(F32)
        d = acc * inv_cnt - ug
        outs.append(jnp.dot(d.astype(BF16), wpool_ref[g], preferred_element_type=F32))
    y = jnp.concatenate(outs, axis=-1) * pscale_ref[...]
    pool_ref[0] = y.astype(BF16)

    @pl.when(t == pl.num_programs(1) - 1)
    def _():
        state_ref[0] = ext_ref[tt + 1:tt + HIST_ROWS]


def _inproj(x, hist16, g_mix, w_in, w_vt, w_pool, pool_scale, *, tt, start_pos, transposed_v):
    B, T, _ = x.shape
    nt = T // tt
    row = lambda b, t: (b, t, 0)
    const2 = lambda b, t: (0, 0)
    wide = lambda dt: jax.ShapeDtypeStruct((B, T, QK_WIDTH), dt)
    if transposed_v:
        vb_spec = pl.BlockSpec((1, tt // KV_BLOCK, QK_WIDTH, KV_BLOCK), lambda b, t: (b, t, 0, 0))
        vb_shape = jax.ShapeDtypeStruct((B, T // KV_BLOCK, QK_WIDTH, KV_BLOCK), BF16)
    else:
        vb_spec = pl.BlockSpec((1, tt, QK_WIDTH), row)
        vb_shape = wide(BF16)
    return pl.pallas_call(
        functools.partial(_inproj_kernel, tt=tt, start_pos=start_pos, transposed_v=transposed_v),
        grid=(B, nt),
        in_specs=[
            pl.BlockSpec((1, tt, D_MODEL), row),
            pl.BlockSpec((1, HIST_ROWS, POOL_WIDTH), lambda b, t: (b, 0, 0)),
            pl.BlockSpec((1, D_MODEL), const2),
            pl.BlockSpec((D_MODEL, PROJ_WIDTH), const2),
            pl.BlockSpec((QK_WIDTH, D_MODEL), const2),
            pl.BlockSpec((len(POOL_WINDOWS), POOL_GROUP, POOL_GROUP), lambda b, t: (0, 0, 0)),
            pl.BlockSpec((1, POOL_WIDTH), const2),
        ],
        out_specs=[
            pl.BlockSpec((1, tt, N_HEADS, HEAD_WIDTH), lambda b, t: (b, t, 0, 0)),
            pl.BlockSpec((1, tt, N_HEADS, HEAD_WIDTH), lambda b, t: (b, t, 0, 0)),
            pl.BlockSpec((1, tt, QK_WIDTH), row),
            pl.BlockSpec((1, tt, QK_WIDTH), row),
            vb_spec,
            pl.BlockSpec((1, tt, POOL_WIDTH), row),
            pl.BlockSpec((1, POOL_HIST, POOL_WIDTH), lambda b, t: (b, 0, 0)),
        ],
        out_shape=[jax.ShapeDtypeStruct((B, T, N_HEADS, HEAD_WIDTH), F32),
                   jax.ShapeDtypeStruct((B, T, N_HEADS, HEAD_WIDTH), F32),
                   wide(BF16), wide(BF16), vb_shape, wide(BF16),
                   jax.ShapeDtypeStruct((B, POOL_HIST, POOL_WIDTH), F32)],
        scratch_shapes=[pltpu.VMEM((HIST_ROWS + tt, POOL_WIDTH), F32)],
        compiler_params=pltpu.CompilerParams(
            dimension_semantics=("parallel", "arbitrary"), vmem_limit_bytes=VMEM_LIMIT),
        name="inproj_pool",
    )(x, hist16, g_mix, w_in, w_vt, w_pool, pool_scale)


def _lambda_value(lq1_ref, lk1_ref, lq2_ref, lk2_ref, li):
    s1 = jnp.sum(lq1_ref[...] * lk1_ref[...], axis=-1, keepdims=True)
    s2 = jnp.sum(lq2_ref[...] * lk2_ref[...], axis=-1, keepdims=True)
    return jnp.exp(s1) - jnp.exp(s2) + li


def _split_halves(q):
    lane = lax.broadcasted_iota(jnp.int32, (1, HEAD_WIDTH), 1)
    zero = jnp.zeros_like(q)
    return jnp.where(lane < HEAD_DIM, q, zero), jnp.where(lane >= HEAD_DIM, q, zero)


def _scores(qh, kblk):
    return lax.dot_general(qh, kblk, (((1,), (1,)), ((), ())), preferred_element_type=F32)


def _subln(o, gsub_ref, li):
    return o * lax.rsqrt(jnp.mean(o * o, axis=-1, keepdims=True) + EPS) * gsub_ref[...] * (1.0 - li)


def _attn_prompt_kernel(lq1_ref, lk1_ref, lq2_ref, lk2_ref, gsub_ref, bias_ref, q_ref, k_ref, vt_ref, o_ref,
                        m_ref, l_ref, acc_ref, *, tq, heads, li):
    tk = KV_BLOCK
    n_sub = tq // Q_SUB
    n_tiles = q_ref.shape[1] // tq
    head_cols = [slice(hh * HEAD_WIDTH, (hh + 1) * HEAD_WIDTH) for hh in range(heads)]

    def query_maps(qi):
        q2 = []
        for cols in head_cols:
            for a in range(n_sub):
                rows = q_ref[0, pl.ds(pl.multiple_of(qi * tq + a * Q_SUB, Q_SUB), Q_SUB), cols]
                q0, q1 = _split_halves(rows)
                q2.append(jnp.concatenate([q0, q1], axis=0))
        return q2

    def qk(q2, j, masked):
        out = []
        for hh, cols in enumerate(head_cols):
            kblk = k_ref[0, pl.ds(pl.multiple_of(j * tk, tk), tk), cols]
            for a in range(n_sub):
                s = _scores(kblk, q2[hh * n_sub + a])
                if masked:
                    s = s + bias_ref[a]
                out.append((s, jnp.max(s, axis=0, keepdims=True)))
        return tuple(out)

    ones_rows = jnp.ones((SEG_ALIGN, tk), BF16)

    def consume(j, sc):
        for hh, cols in enumerate(head_cols):
            vtb = jnp.concatenate([vt_ref[0, j, cols, :], ones_rows], axis=0)
            for a in range(n_sub):
                c = hh * n_sub + a
                s, cmax = sc[c]
                m_old = m_ref[c]
                m_new = jnp.maximum(m_old, cmax)
                alpha = jnp.exp2(m_old - m_new)
                p = jnp.exp2(s - m_new)
                pv = jnp.dot(vtb, p.astype(BF16), preferred_element_type=F32)
                l_ref[c] = alpha * l_ref[c] + pv[HEAD_WIDTH:HEAD_WIDTH + 1]
                acc_ref[c] = alpha * acc_ref[c] + pv[:HEAD_WIDTH]
                m_ref[c] = m_new

    lam = _lambda_value(lq1_ref, lk1_ref, lq2_ref, lk2_ref, li)
    scale = gsub_ref[...] * (1.0 - li)

    def tile(qi, sc):
        q2 = query_maps(qi)
        m_ref[...] = jnp.full(m_ref.shape, NEG_INF, F32)
        l_ref[...] = jnp.zeros(l_ref.shape, F32)
        acc_ref[...] = jnp.zeros(acc_ref.shape, F32)

        def body(j, carry):
            prev, cur = carry
            nxt = qk(q2, j, False)
            consume(prev, cur)
            return j, nxt

        last, cur = lax.fori_loop(0, qi, body, (qi, sc))
        qn = jnp.minimum(qi + 1, n_tiles - 1)
        nxt = qk(query_maps(qn), qn, True)
        consume(last, cur)
        for hh, cols in enumerate(head_cols):
            for a in range(n_sub):
                c = hh * n_sub + a
                o2 = acc_ref[c] / l_ref[c]
                o_t = o2[:, :Q_SUB] - lam * o2[:, Q_SUB:]
                o_t = o_t * lax.rsqrt(jnp.mean(o_t * o_t, axis=0, keepdims=True) + EPS)
                rows = pl.ds(pl.multiple_of(qi * tq + a * Q_SUB, Q_SUB), Q_SUB)
                o_ref[0, rows, cols] = (o_t.T * scale).astype(BF16)
        return nxt

    lax.fori_loop(0, n_tiles, tile, qk(query_maps(0), 0, True))


def _attn_prompt(lams, g_subln, qb, kb, vt, *, tq, heads, li):
    B, T, _ = qb.shape
    assert tq == KV_BLOCK and tq % Q_SUB == 0
    key_chunk = jnp.arange(KV_BLOCK)[None, :, None] // CHUNK
    query = jnp.arange(tq // Q_SUB)[:, None, None] * Q_SUB + jnp.arange(2 * Q_SUB)[None, None, :] % Q_SUB
    bias = jnp.where(key_chunk <= query // CHUNK, 0.0, NEG_INF).astype(F32)
    c2 = lambda b, h: (0, 0)
    small = pl.BlockSpec((1, HEAD_DIM), c2)
    width = heads * HEAD_WIDTH
    chains = heads * (tq // Q_SUB)
    head = pl.BlockSpec((1, T, width), lambda b, h: (b, 0, h))
    return pl.pallas_call(
        functools.partial(_attn_prompt_kernel, tq=tq, heads=heads, li=li),
        grid=(B, N_HEADS // heads),
        in_specs=[small, small, small, small,
                  pl.BlockSpec((1, HEAD_WIDTH), c2),
                  pl.BlockSpec(bias.shape, lambda b, h: (0, 0, 0)),
                  head, head,
                  pl.BlockSpec((1, T // KV_BLOCK, width, KV_BLOCK), lambda b, h: (b, 0, h, 0))],
        out_specs=head,
        out_shape=jax.ShapeDtypeStruct((B, T, QK_WIDTH), BF16),
        scratch_shapes=[pltpu.VMEM((chains, 1, 2 * Q_SUB), F32),
                        pltpu.VMEM((chains, 1, 2 * Q_SUB), F32),
                        pltpu.VMEM((chains, HEAD_WIDTH, 2 * Q_SUB), F32)],
        compiler_params=pltpu.CompilerParams(
            dimension_semantics=("parallel", "parallel"), vmem_limit_bytes=VMEM_LIMIT),
        name="attn_prompt",
    )(*lams, g_subln, bias, qb, kb, vt)


def _attn_sample_kernel(lq1_ref, lk1_ref, lq2_ref, lk2_ref, gsub_ref, q_ref, kc_ref, vc_ref,
                        kn_ref, vn_ref, o_ref, *, li):
    T = q_ref.shape[1]
    lam = _lambda_value(lq1_ref, lk1_ref, lq2_ref, lk2_ref, li)
    for hd in range(N_HEADS):
        cols = slice(hd * HEAD_WIDTH, (hd + 1) * HEAD_WIDTH)
        q2 = jnp.concatenate(_split_halves(q_ref[0, :, cols]), axis=0)
        kc = kc_ref[0, :, hd, :].astype(BF16)
        vc = vc_ref[0, :, hd, :].astype(BF16)
        sc = _scores(q2, kc)
        sn = _scores(q2, kn_ref[0, :, cols])
        m = jnp.maximum(jnp.max(sc, axis=-1, keepdims=True), jnp.max(sn, axis=-1, keepdims=True))
        pc = jnp.exp(sc - m)
        pn = jnp.exp(sn - m)
        l = jnp.sum(pc, axis=-1, keepdims=True) + jnp.sum(pn, axis=-1, keepdims=True)
        acc = (jnp.dot(pc.astype(BF16), vc, preferred_element_type=F32)
               + jnp.dot(pn.astype(BF16), vn_ref[0, :, cols], preferred_element_type=F32))
        o2 = acc / l
        o = o2[:T] - lam * o2[T:]
        o_ref[0, :, cols] = _subln(o, gsub_ref, li).astype(BF16)


def _attn_sample(lams, g_subln, qb, cache_k, cache_v, kb, vb, *, li):
    B, T, _ = qb.shape
    P = cache_k.shape[1]
    c2 = lambda b: (0, 0)
    small = pl.BlockSpec((1, HEAD_DIM), c2)
    rows = pl.BlockSpec((1, T, QK_WIDTH), lambda b: (b, 0, 0))
    stream = pl.BlockSpec((1, P, N_HEADS, HEAD_WIDTH), lambda b: (b, 0, 0, 0))
    return pl.pallas_call(
        functools.partial(_attn_sample_kernel, li=li),
        grid=(B,),
        in_specs=[small, small, small, small, pl.BlockSpec((1, HEAD_WIDTH), c2),
                  rows, stream, stream, rows, rows],
        out_specs=rows,
        out_shape=jax.ShapeDtypeStruct((B, T, QK_WIDTH), BF16),
        compiler_params=pltpu.CompilerParams(
            dimension_semantics=("parallel",), vmem_limit_bytes=VMEM_LIMIT),
        name="attn_sample",
    )(*lams, g_subln, qb, cache_k, cache_v, kb, vb)


def _bf16_split(x):
    hi = x.astype(BF16)
    return hi, (x - hi.astype(F32)).astype(BF16)


def _route_sort_kernel(x_ref, pool_ref, attn_ref, wout_ref, gffn_ref, wrt_ref, brt_ref,
                       x1_ref, route_ref, seg_ref, hs_ref,
                       stage_ref, zero_ref, run_ref, sem_ref, *, tm, sub, cap, nsteps):
    i = pl.program_id(0)
    slot = lax.rem(i, 2)
    slots = tm + (tm // sub) * N_GROUPS * SEG_ALIGN

    @pl.when(i == 0)
    def _():
        for g in range(N_GROUPS):
            run_ref[g] = 0
        stage_ref[...] = jnp.zeros(stage_ref.shape, BF16)
        zero_ref[...] = jnp.zeros(zero_ref.shape, BF16)

    mix = (jnp.dot(pool_ref[...], wout_ref[0:POOL_WIDTH, :], preferred_element_type=F32)
           + jnp.dot(attn_ref[...], wout_ref[POOL_WIDTH:, :], preferred_element_type=F32))
    x1 = x_ref[...] + mix
    x1_ref[...] = x1
    ms = jnp.mean(x1 * x1, axis=-1, keepdims=True)
    h2 = x1 * lax.rsqrt(ms + EPS) * gffn_ref[...]
    h_hi, h_lo = _bf16_split(h2)
    nt = (((1,), (1,)), ((), ()))
    hi_lo = lax.dot_general(wrt_ref[...], h_hi, nt, preferred_element_type=F32)
    logits = (hi_lo[:LANES] + hi_lo[LANES:]
              + lax.dot_general(wrt_ref[0:LANES, :], h_lo, nt, preferred_element_type=F32)
              + brt_ref[...])
    row = lambda r: logits[r:r + 1, :]

    def first_argmax(vals):
        best = functools.reduce(jnp.maximum, vals)
        idx = jnp.full(best.shape, len(vals) - 1, jnp.int32)
        for k in range(len(vals) - 2, -1, -1):
            idx = jnp.where(vals[k] == best, k, idx)
        return best, idx

    group_logits = [row(g) for g in range(N_GROUPS)]
    gmax, g_idx = first_argmax(group_logits)
    g_gate = 1.0 / functools.reduce(jnp.add, [jnp.exp(v - gmax) for v in group_logits])
    in_group = [g_idx == g for g in range(N_GROUPS)]
    e_sel = []
    for k in range(EXPERTS_PER_GROUP):
        v = row(ROUTER_LANE0 + (N_GROUPS - 1) * EXPERTS_PER_GROUP + k)
        for g in range(N_GROUPS - 2, -1, -1):
            v = jnp.where(in_group[g], row(ROUTER_LANE0 + g * EXPERTS_PER_GROUP + k), v)
        e_sel.append(v)
    v1, i1 = first_argmax(e_sel)
    v2, i2 = first_argmax([jnp.where(i1 == k, -jnp.inf, e_sel[k]) for k in range(EXPERTS_PER_GROUP)])
    e21 = jnp.exp(v2 - v1)
    w1 = g_gate / (1.0 + e21)
    w2 = w1 * e21
    w1h = w1.astype(BF16).astype(F32)
    w2h = w2.astype(BF16).astype(F32)

    onehot = [jnp.where(m, 1.0, 0.0) for m in in_group]
    onehot8 = jnp.concatenate(onehot + [jnp.zeros((8 - N_GROUPS, tm), F32)], axis=0)
    earlier = lax.broadcasted_iota(jnp.int32, (tm, tm), 0)
    token = lax.broadcasted_iota(jnp.int32, (tm, tm), 1)
    before = (earlier < token) & (earlier // sub == token // sub)
    rank_all = jnp.dot(onehot8.astype(BF16), jnp.where(before, 1.0, 0.0).astype(BF16),
                       preferred_element_type=F32)
    rank = functools.reduce(jnp.add, [onehot[g] * rank_all[g:g + 1] for g in range(N_GROUPS)])
    n_sub = tm // sub
    n_pad = []
    for s in range(n_sub):
        n = jnp.sum(onehot8[:, s * sub:(s + 1) * sub], axis=1, keepdims=True)
        n_pad.append(jnp.floor((n + (SEG_ALIGN - 1)) * (1.0 / SEG_ALIGN)) * SEG_ALIGN)
    sub_of_token = lax.broadcasted_iota(jnp.int32, (1, tm), 1) // sub
    off = {}
    acc = jnp.zeros((1, 1), F32)
    local = rank
    for g in range(N_GROUPS):
        for s in range(n_sub):
            off[g, s] = acc
            local = local + jnp.where(in_group[g] & (sub_of_token == s), acc, 0.0)
            acc = acc + n_pad[s][g:g + 1]

    hi = [jnp.where(i1 == k, w1h, 0.0) + jnp.where(i2 == k, w2h, 0.0) for k in range(EXPERTS_PER_GROUP)]
    lo = [jnp.where(i1 == k, w1 - w1h, 0.0) + jnp.where(i2 == k, w2 - w2h, 0.0)
          for k in range(EXPERTS_PER_GROUP)]
    where = g_idx.astype(F32) * sub + rank
    per_token = jnp.concatenate(hi + lo + [where, jnp.zeros((LANES - ROUTE_LANE - 1, tm), F32)], axis=0).T
    route_ref[...] = per_token

    place = jnp.where(lax.broadcasted_iota(jnp.int32, (slots, tm), 0).astype(F32) == local, 1.0, 0.0)
    rows = jnp.concatenate([h_hi, per_token.astype(BF16)], axis=1)
    sorted_rows = jnp.dot(place.astype(BF16), rows, preferred_element_type=F32)
    stage_ref[slot, 0:slots, :] = sorted_rows.astype(BF16)

    def window(g, slot_, src_row, dst_row):
        return pltpu.make_async_copy(
            stage_ref.at[slot_, pl.ds(pl.multiple_of(src_row, SEG_ALIGN), tm)],
            hs_ref.at[pl.ds(pl.multiple_of(dst_row, SEG_ALIGN), tm)],
            sem_ref.at[slot_])

    @pl.when(i > 0)
    def _():
        for g in range(N_GROUPS):
            window(g, 1 - slot, 0, 0).wait()

    for g in range(N_GROUPS):
        src = off[g, 0][0, 0].astype(jnp.int32)
        start = run_ref[g]
        window(g, slot, src, g * cap + start).start()
        for s in range(n_sub):
            seg_ref[g, n_sub * i + s] = start
            start = start + n_pad[s][g, 0].astype(jnp.int32)
        run_ref[g] = start

    @pl.when(i == nsteps - 1)
    def _():
        for g in range(N_GROUPS):
            window(g, slot, 0, 0).wait()
        fills = []
        for g in range(N_GROUPS):
            total = run_ref[g]
            seg_ref[g, n_sub * nsteps] = total
            for z in range(ZERO_FILL_ROWS // ZERO_ROWS):
                dst = g * cap + total + z * ZERO_ROWS
                fills.append(pltpu.make_async_copy(
                    zero_ref, hs_ref.at[pl.ds(pl.multiple_of(dst, SEG_ALIGN), ZERO_ROWS)],
                    sem_ref.at[2]))
        for f in fills:
            f.start()
        for f in fills:
            f.wait()


def _round_up(x, m):
    return (x + m - 1) // m * m


def _sorted_capacity(n_tokens, ntiles):
    return _round_up(n_tokens + SEG_ALIGN * ntiles, MOE_BLOCK) + 3 * MOE_BLOCK


def _route_sort(x2d, pool2d, attn2d, w_out, g_ffn, wrt, brt, *, tm, sub):
    N = x2d.shape[0]
    nsteps = N // tm
    ntiles = N // sub
    cap = _sorted_capacity(N, ntiles)
    row = lambda i: (i, 0)
    c2 = lambda i: (0, 0)
    return pl.pallas_call(
        functools.partial(_route_sort_kernel, tm=tm, sub=sub, cap=cap, nsteps=nsteps),
        grid=(nsteps,),
        in_specs=[pl.BlockSpec((tm, D_MODEL), row),
                  pl.BlockSpec((tm, POOL_WIDTH), row),
                  pl.BlockSpec((tm, QK_WIDTH), row),
                  pl.BlockSpec((D_MODEL, D_MODEL), c2),
                  pl.BlockSpec((1, D_MODEL), c2),
                  pl.BlockSpec((2 * LANES, D_MODEL), c2),
                  pl.BlockSpec((LANES, 1), c2)],
        out_specs=[pl.BlockSpec((tm, D_MODEL), row),
                   pl.BlockSpec((tm, LANES), row),
                   pl.BlockSpec(memory_space=pltpu.SMEM),
                   pl.BlockSpec(memory_space=pl.ANY)],
        out_shape=[jax.ShapeDtypeStruct((N, D_MODEL), F32),
                   jax.ShapeDtypeStruct((N, LANES), F32),
                   jax.ShapeDtypeStruct((N_GROUPS, ntiles + 1), jnp.int32),
                   jax.ShapeDtypeStruct((N_GROUPS * cap, ROW_W), BF16)],
        scratch_shapes=[pltpu.VMEM((2, 2 * tm + (tm // sub) * N_GROUPS * SEG_ALIGN, ROW_W), BF16),
                        pltpu.VMEM((ZERO_ROWS, ROW_W), BF16),
                        pltpu.SMEM((N_GROUPS,), jnp.int32),
                        pltpu.SemaphoreType.DMA((3,))],
        compiler_params=pltpu.CompilerParams(
            dimension_semantics=("arbitrary",), vmem_limit_bytes=VMEM_LIMIT),
        name="route_sort",
    )(x2d, pool2d, attn2d, w_out, g_ffn, wrt, brt)


def _group_moe_kernel(blk_ref, grp_ref, valid_ref, hs_ref, wg_ref, wu_ref, wd_ref, out_ref):
    b = pl.program_id(0)

    @pl.when(valid_ref[b] == 1)
    def _():
        rows = hs_ref[...]
        x = rows[:, :D_MODEL]
        cparts = rows[:, D_MODEL:].astype(F32)
        lane = lax.broadcasted_iota(jnp.int32, cparts.shape, 1)
        hes = []
        for e in range(EXPERTS_PER_GROUP):
            a = jnp.dot(x, wg_ref[e], preferred_element_type=F32)
            u = jnp.dot(x, wu_ref[e], preferred_element_type=F32)
            c = jnp.sum(jnp.where((lane == e) | (lane == e + EXPERTS_PER_GROUP), cparts, 0.0),
                        axis=-1, keepdims=True)
            hes.append((a * jax.nn.sigmoid(a) * u * c).astype(BF16))
        he = jnp.concatenate(hes, axis=1)
        out_ref[...] = jnp.dot(he, wd_ref[0], preferred_element_type=F32).astype(BF16)

    @pl.when(valid_ref[b] == 0)
    def _():
        out_ref[...] = jnp.zeros(out_ref.shape, BF16)


def _group_moe(blk, grp, valid, hs, w_gate, w_up, w_down_grouped):
    nb = blk.shape[0]
    bm = MOE_BLOCK
    by_block = lambda b, blk, grp, valid: (blk[b], 0)
    by_group = lambda b, blk, grp, valid: (grp[b], 0, 0)
    grid_spec = pltpu.PrefetchScalarGridSpec(
        num_scalar_prefetch=3,
        grid=(nb,),
        in_specs=[pl.BlockSpec((bm, ROW_W), by_block),
                  pl.BlockSpec((EXPERTS_PER_GROUP, D_MODEL, D_EXPERT), by_group),
                  pl.BlockSpec((EXPERTS_PER_GROUP, D_MODEL, D_EXPERT), by_group),
                  pl.BlockSpec((1, EXPERTS_PER_GROUP * D_EXPERT, D_MODEL), by_group)],
        out_specs=pl.BlockSpec((bm, D_MODEL), by_block),
    )
    return pl.pallas_call(
        _group_moe_kernel,
        grid_spec=grid_spec,
        out_shape=jax.ShapeDtypeStruct((hs.shape[0], D_MODEL), BF16),
        compiler_params=pltpu.CompilerParams(
            dimension_semantics=("arbitrary",), vmem_limit_bytes=VMEM_LIMIT),
        name="group_moe",
    )(blk, grp, valid, hs, w_gate, w_up, w_down_grouped)


def _block_table(seg, n_tokens, ntiles, cap):
    bm = MOE_BLOCK
    nb_max = _round_up(n_tokens + N_GROUPS * SEG_ALIGN * ntiles, bm) // bm + 2 * N_GROUPS
    total = seg[:, ntiles]
    nb = (total + bm - 1) // bm + 1
    end = jnp.cumsum(nb)
    start = end - nb
    b = jnp.arange(nb_max, dtype=jnp.int32)
    g = jnp.sum((b[:, None] >= end[None, :]).astype(jnp.int32), axis=1)
    valid = g < N_GROUPS
    gc = jnp.minimum(g, N_GROUPS - 1)
    blk = jnp.where(valid, gc * (cap // bm) + b - start[gc], N_GROUPS * cap // bm - 1)
    return blk.astype(jnp.int32), gc.astype(jnp.int32), valid.astype(jnp.int32)


def _unsort_norm_kernel(seg_ref, x1_ref, route_ref, gfin_ref, os_ref, y_ref, win_ref, sem_ref,
                        *, tm, cap, ntiles):
    i = pl.program_id(0)
    slot = lax.rem(i, 2)

    def window(tile, g, slot_):
        start = seg_ref[g * (ntiles + 1) + tile]
        return pltpu.make_async_copy(
            os_ref.at[pl.ds(pl.multiple_of(g * cap + start, SEG_ALIGN), tm)],
            win_ref.at[slot_, g], sem_ref.at[slot_])

    @pl.when(i == 0)
    def _():
        for g in range(N_GROUPS):
            window(0, g, 0).start()

    @pl.when(i + 1 < ntiles)
    def _():
        for g in range(N_GROUPS):
            window(i + 1, g, 1 - slot).start()

    for g in range(N_GROUPS):
        window(i, g, slot).wait()

    where = route_ref[:, ROUTE_LANE:ROUTE_LANE + 1]
    pick = jnp.where(lax.broadcasted_iota(jnp.int32, (tm, N_GROUPS * tm), 1).astype(F32) == where, 1.0, 0.0)
    moe = jnp.dot(pick.astype(BF16), win_ref[slot].reshape(N_GROUPS * tm, D_MODEL),
                  preferred_element_type=F32)
    x2 = x1_ref[...] + moe
    ms = jnp.mean(x2 * x2, axis=-1, keepdims=True)
    y_ref[...] = x2 * lax.rsqrt(ms + EPS) * gfin_ref[...]


def _unsort_norm(seg, x1, route, g_final, out_sorted, *, tm, cap):
    N = x1.shape[0]
    ntiles = N // tm
    grid_spec = pltpu.PrefetchScalarGridSpec(
        num_scalar_prefetch=1,
        grid=(ntiles,),
        in_specs=[pl.BlockSpec((tm, D_MODEL), lambda i, seg: (i, 0)),
                  pl.BlockSpec((tm, LANES), lambda i, seg: (i, 0)),
                  pl.BlockSpec((1, D_MODEL), lambda i, seg: (0, 0)),
                  pl.BlockSpec(memory_space=pl.ANY)],
        out_specs=pl.BlockSpec((tm, D_MODEL), lambda i, seg: (i, 0)),
        scratch_shapes=[pltpu.VMEM((2, N_GROUPS, tm, D_MODEL), BF16),
                        pltpu.SemaphoreType.DMA((2,))],
    )
    return pl.pallas_call(
        functools.partial(_unsort_norm_kernel, tm=tm, cap=cap, ntiles=ntiles),
        grid_spec=grid_spec,
        out_shape=jax.ShapeDtypeStruct((N, D_MODEL), F32),
        compiler_params=pltpu.CompilerParams(
            dimension_semantics=("arbitrary",), vmem_limit_bytes=VMEM_LIMIT),
        name="unsort_norm",
    )(seg.reshape(-1), x1, route, g_final, out_sorted)


def _layer(x, hist16, cache, start_pos, layer, w, g_final, *, tt, tm):
    B, T, _ = x.shape
    li = _lambda_init(layer)
    k, v, qb, kb, vb, pool, state = _inproj(x, hist16, w["g_mix"], w["w_in"], w["w_vt"], w["w_pool"],
                                            w["pool_scale"], tt=tt, start_pos=start_pos,
                                            transposed_v=cache is None)
    if cache is None:
        attn = _attn_prompt(w["lams"], w["g_subln"], qb, kb, vb, tq=KV_BLOCK, heads=ATTN_HEADS_PER_STEP, li=li)
    else:
        attn = _attn_sample(w["lams"], w["g_subln"], qb, cache[0], cache[1], kb, vb, li=li)
    N = B * T
    ntiles = N // tm
    cap = _sorted_capacity(N, ntiles)
    x1, route, seg, hs = _route_sort(x.reshape(N, D_MODEL), pool.reshape(N, POOL_WIDTH),
                                     attn.reshape(N, QK_WIDTH), w["w_out"], w["g_ffn"],
                                     w["wrt"], w["brt"], tm=ROUTE_STEP, sub=tm)
    blk, grp, valid = _block_table(seg, N, ntiles, cap)
    out_sorted = _group_moe(blk, grp, valid, hs, w["w_gate"], w["w_up"], w["w_down"])
    y = _unsort_norm(seg, x1, route, g_final, out_sorted, tm=tm, cap=cap)
    return y.reshape(B, T, D_MODEL), k, v, state


def kernel(x_prompt, x_sample, cache_k, cache_v, state_pool, g_mix, w_in, w_pool, pool_scale, lam_q1, lam_k1, lam_q2, lam_k2, g_subln, w_out, g_ffn, w_group, b_group, w_erouter, b_erouter, w_gate, w_up, w_down, g_final):
    depth = g_mix.shape[0]
    assert depth == 1, "the final norm is fused into the layer, so exactly one layer is supported"
    l = 0
    wr = jnp.concatenate([w_group[l], w_erouter[l]], axis=1).T
    wr = jnp.pad(wr, ((0, LANES - wr.shape[0]), (0, 0)))
    wr_hi = wr.astype(BF16)
    wrt = jnp.concatenate([wr_hi, (wr - wr_hi.astype(F32)).astype(BF16)], axis=0)
    br = jnp.pad(jnp.concatenate([b_group[l], b_erouter[l]]), (0, LANES - N_GROUPS - N_EXPERTS))
    w = dict(
        g_mix=g_mix[l][None], w_in=w_in[l].astype(BF16),
        w_vt=w_in[l][:, POOL_WIDTH + 2 * QK_WIDTH:].T.astype(BF16), w_pool=w_pool[l].astype(BF16),
        pool_scale=pool_scale[l][None],
        lams=(lam_q1[l][None], lam_k1[l][None], lam_q2[l][None], lam_k2[l][None]),
        g_subln=g_subln[l][None], w_out=w_out[l].astype(BF16), g_ffn=g_ffn[l][None],
        wrt=wrt, brt=br[:, None],
        w_gate=w_gate[l].astype(BF16), w_up=w_up[l].astype(BF16),
        w_down=w_down[l].astype(BF16).reshape(N_GROUPS, EXPERTS_PER_GROUP * D_EXPERT, D_MODEL),
    )
    gfin = g_final[None]
    Bp = x_prompt.shape[0]
    Bs, _, _ = x_sample.shape
    P = cache_k.shape[2]
    zero_hist = jnp.zeros((Bp, HIST_ROWS, POOL_WIDTH), F32)
    samp_hist = jnp.pad(state_pool[l], ((0, 0), (HIST_ROWS - POOL_HIST, 0), (0, 0)))
    cache = (cache_k[l], cache_v[l])

    yp, kp, vp, pp = _layer(x_prompt, zero_hist, None, 0, l, w, gfin, tt=512, tm=ROUTE_TILE)
    ys, kn, vn, pn = _layer(x_sample, samp_hist, cache, P, l, w, gfin, tt=64, tm=ROUTE_TILE)
    return (yp, ys, kp[None], vp[None], pp[None], kn[None], vn[None], pn[None])
```

```python
import functools
import math

import jax
import jax.numpy as jnp
from jax import lax
from jax.experimental import pallas as pl
from jax.experimental.pallas import tpu as pltpu

F32 = jnp.float32
BF16 = jnp.bfloat16

D_MODEL = 1024
CHUNK = 64
POOL_WIDTH = 512
POOL_WINDOWS = (2, 4, 8, 16)
POOL_GROUP = 128
POOL_HIST = 15
HIST_ROWS = 16
N_HEADS = 4
HEAD_DIM = 64
HEAD_WIDTH = 128
QK_WIDTH = 512
PROJ_WIDTH = 2048
ATTN_SCALE = HEAD_DIM ** -0.5
LOG2_E = math.log2(math.e)
NEG_INF = -1e30
N_GROUPS = 4
EXPERTS_PER_GROUP = 4
N_EXPERTS = 16
D_EXPERT = 256
EPS = 1e-6
LANES = 128
KV_BLOCK = 256
Q_SUB = 128
ATTN_HEADS_PER_STEP = 2
ROUTER_LANE0 = N_GROUPS
ROUTE_TILE = 256
ROUTE_STEP = 512
ROUTE_LANE = 8
SEG_ALIGN = 16
ROW_W = D_MODEL + LANES
MOE_BLOCK = 512
ZERO_ROWS = 256
ZERO_FILL_ROWS = 2 * MOE_BLOCK
VMEM_LIMIT = 48 * 1024 * 1024


def _lambda_init(layer):
    return 0.8 - 0.6 * math.exp(-0.3 * layer)


def _inproj_kernel(x_ref, hist_ref, gmix_ref, win_ref, wvt_ref, wpool_ref, pscale_ref,
                   k_ref, v_ref, qb_ref, kb_ref, vb_ref, pool_ref, state_ref,
                   ext_ref, *, tt, start_pos, transposed_v):
    t = pl.program_id(1)
    x = x_ref[0]
    ms = jnp.mean(x * x, axis=-1, keepdims=True)
    h = (x * lax.rsqrt(ms + EPS) * gmix_ref[...]).astype(BF16)
    proj = jnp.dot(h, win_ref[...], preferred_element_type=F32)
    u = proj[:, :POOL_WIDTH]
    q = proj[:, POOL_WIDTH:POOL_WIDTH + QK_WIDTH]
    k = proj[:, POOL_WIDTH + QK_WIDTH:POOL_WIDTH + 2 * QK_WIDTH]
    v = proj[:, POOL_WIDTH + 2 * QK_WIDTH:]
    for hd in range(N_HEADS):
        k_ref[0, :, hd, :] = k[:, hd * HEAD_WIDTH:(hd + 1) * HEAD_WIDTH]
        v_ref[0, :, hd, :] = v[:, hd * HEAD_WIDTH:(hd + 1) * HEAD_WIDTH]
    qb_ref[0] = (q * (ATTN_SCALE * LOG2_E if transposed_v else ATTN_SCALE)).astype(BF16)
    kb_ref[0] = k.astype(BF16)
    if transposed_v:
        vt = lax.dot_general(wvt_ref[...], h, (((1,), (1,)), ((), ())),
                             preferred_element_type=F32).astype(BF16)
        for i in range(tt // KV_BLOCK):
            vb_ref[0, i] = vt[:, i * KV_BLOCK:(i + 1) * KV_BLOCK]
    else:
        vb_ref[0] = v.astype(BF16)

    @pl.when(t == 0)
    def _():
        ext_ref[0:HIST_ROWS] = hist_ref[0]

    @pl.when(t > 0)
    def _():
        ext_ref[0:HIST_ROWS] = ext_ref[tt:tt + HIST_ROWS]

    ext_ref[HIST_ROWS:HIST_ROWS + tt] = u

    pos = start_pos + t * tt + lax.broadcasted_iota(jnp.int32, (tt, 1), 0)
    outs = []
    for g, w in enumerate(POOL_WINDOWS):
        lo = g * POOL_GROUP
        ug = u[:, lo:lo + POOL_GROUP]
        acc = ug
        for j in range(1, w):
            acc = acc + ext_ref[HIST_ROWS - j:HIST_ROWS - j + tt, lo:lo + POOL_GROUP]
        inv_cnt = 1.0 / jnp.minimum(w, pos + 1).astype(F32)
        d = acc * inv_cnt - ug
        outs.append(jnp.dot(d.astype(BF16), wpool_ref[g], preferred_element_type=F32))
    y = jnp.concatenate(outs, axis=-1) * pscale_ref[...]
    pool_ref[0] = y.astype(BF16)

    @pl.when(t == pl.num_programs(1) - 1)
    def _():
        state_ref[0] = ext_ref[tt + 1:tt + HIST_ROWS]


def _inproj(x, hist16, g_mix, w_in, w_vt, w_pool, pool_scale, *, tt, start_pos, transposed_v):
    B, T, _ = x.shape
    nt = T // tt
    row = lambda b, t: (b, t, 0)
    const2 = lambda b, t: (0, 0)
    wide = lambda dt: jax.ShapeDtypeStruct((B, T, QK_WIDTH), dt)
    if transposed_v:
        vb_spec = pl.BlockSpec((1, tt // KV_BLOCK, QK_WIDTH, KV_BLOCK), lambda b, t: (b, t, 0, 0))
        vb_shape = jax.ShapeDtypeStruct((B, T // KV_BLOCK, QK_WIDTH, KV_BLOCK), BF16)
    else:
        vb_spec = pl.BlockSpec((1, tt, QK_WIDTH), row)
        vb_shape = wide(BF16)
    return pl.pallas_call(
        functools.partial(_inproj_kernel, tt=tt, start_pos=start_pos, transposed_v=transposed_v),
        grid=(B, nt),
        in_specs=[
            pl.BlockSpec((1, tt, D_MODEL), row),
            pl.BlockSpec((1, HIST_ROWS, POOL_WIDTH), lambda b, t: (b, 0, 0)),
            pl.BlockSpec((1, D_MODEL), const2),
            pl.BlockSpec((D_MODEL, PROJ_WIDTH), const2),
            pl.BlockSpec((QK_WIDTH, D_MODEL), const2),
            pl.BlockSpec((len(POOL_WINDOWS), POOL_GROUP, POOL_GROUP), lambda b, t: (0, 0, 0)),
            pl.BlockSpec((1, POOL_WIDTH), const2),
        ],
        out_specs=[
            pl.BlockSpec((1, tt, N_HEADS, HEAD_WIDTH), lambda b, t: (b, t, 0, 0)),
            pl.BlockSpec((1, tt, N_HEADS, HEAD_WIDTH), lambda b, t: (b, t, 0, 0)),
            pl.BlockSpec((1, tt, QK_WIDTH), row),
            pl.BlockSpec((1, tt, QK_WIDTH), row),
            vb_spec,
            pl.BlockSpec((1, tt, POOL_WIDTH), row),
            pl.BlockSpec((1, POOL_HIST, POOL_WIDTH), lambda b, t: (b, 0, 0)),
        ],
        out_shape=[jax.ShapeDtypeStruct((B, T, N_HEADS, HEAD_WIDTH), F32),
                   jax.ShapeDtypeStruct((B, T, N_HEADS, HEAD_WIDTH), F32),
                   wide(BF16), wide(BF16), vb_shape, wide(BF16),
                   jax.ShapeDtypeStruct((B, POOL_HIST, POOL_WIDTH), F32)],
        scratch_shapes=[pltpu.VMEM((HIST_ROWS + tt, POOL_WIDTH), F32)],
        compiler_params=pltpu.CompilerParams(
            dimension_semantics=("parallel", "arbitrary"), vmem_limit_bytes=VMEM_LIMIT),
        name="inproj_pool",
    )(x, hist16, g_mix, w_in, w_vt, w_pool, pool_scale)


def _lambda_value(lq1_ref, lk1_ref, lq2_ref, lk2_ref, li):
    s1 = jnp.sum(lq1_ref[...] * lk1_ref[...], axis=-1, keepdims=True)
    s2 = jnp.sum(lq2_ref[...] * lk2_ref[...], axis=-1, keepdims=True)
    return jnp.exp(s1) - jnp.exp(s2) + li


def _split_halves(q):
    lane = lax.broadcasted_iota(jnp.int32, (1, HEAD_WIDTH), 1)
    zero = jnp.zeros_like(q)
    return jnp.where(lane < HEAD_DIM, q, zero), jnp.where(lane >= HEAD_DIM, q, zero)


def _scores(qh, kblk):
    return lax.dot_general(qh, kblk, (((1,), (1,)), ((), ())), preferred_element_type=F32)


def _subln(o, gsub_ref, li):
    return o * lax.rsqrt(jnp.mean(o * o, axis=-1, keepdims=True) + EPS) * gsub_ref[...] * (1.0 - li)


def _attn_prompt_kernel(lq1_ref, lk1_ref, lq2_ref, lk2_ref, gsub_ref, bias_ref, q_ref, k_ref, vt_ref, o_ref,
                        m_ref, l_ref, acc_ref, *, tq, heads, li):
    tk = KV_BLOCK
    n_sub = tq // Q_SUB
    n_tiles = q_ref.shape[1] // tq
    head_cols = [slice(hh * HEAD_WIDTH, (hh + 1) * HEAD_WIDTH) for hh in range(heads)]

    def query_maps(qi):
        q2 = []
        for cols in head_cols:
            for a in range(n_sub):
                rows = q_ref[0, pl.ds(pl.multiple_of(qi * tq + a * Q_SUB, Q_SUB), Q_SUB), cols]
                q0, q1 = _split_halves(rows)
                q2.append(jnp.concatenate([q0, q1], axis=0))
        return q2

    def qk(q2, j, masked):
        out = []
        for hh, cols in enumerate(head_cols):
            kblk = k_ref[0, pl.ds(pl.multiple_of(j * tk, tk), tk), cols]
            for a in range(n_sub):
                s = _scores(kblk, q2[hh * n_sub + a])
                if masked:
                    s = s + bias_ref[a]
                out.append((s, jnp.max(s, axis=0, keepdims=True)))
        return tuple(out)

    ones_rows = jnp.ones((SEG_ALIGN, tk), BF16)

    def consume(j, sc):
        for hh, cols in enumerate(head_cols):
            vtb = jnp.concatenate([vt_ref[0, j, cols, :], ones_rows], axis=0)
            for a in range(n_sub):
                c = hh * n_sub + a
                s, cmax = sc[c]
                m_old = m_ref[c]
                m_new = jnp.maximum(m_old, cmax)
                alpha = jnp.exp2(m_old - m_new)
                p = jnp.exp2(s - m_new)
                pv = jnp.dot(vtb, p.astype(BF16), preferred_element_type=F32)
                l_ref[c] = alpha * l_ref[c] + pv[HEAD_WIDTH:HEAD_WIDTH + 1]
                acc_ref[c] = alpha * acc_ref[c] + pv[:HEAD_WIDTH]
                m_ref[c] = m_new

    lam = _lambda_value(lq1_ref, lk1_ref, lq2_ref, lk2_ref, li)
    scale = gsub_ref[...] * (1.0 - li)

    def tile(qi, sc):
        q2 = query_maps(qi)
        m_ref[...] = jnp.full(m_ref.shape, NEG_INF, F32)
        l_ref[...] = jnp.zeros(l_ref.shape, F32)
        acc_ref[...] = jnp.zeros(acc_ref.shape, F32)

        def body(j, carry):
            prev, cur = carry
            nxt = qk(q2, j, False)
            consume(prev, cur)
            return j, nxt

        last, cur = lax.fori_loop(0, qi, body, (qi, sc))
        qn = jnp.minimum(qi + 1, n_tiles - 1)
        nxt = qk(query_maps(qn), qn, True)
        consume(last, cur)
        for hh, cols in enumerate(head_cols):
            for a in range(n_sub):
                c = hh * n_sub + a
                o2 = acc_ref[c] / l_ref[c]
                o_t = o2[:, :Q_SUB] - lam * o2[:, Q_SUB:]
                o_t = o_t * lax.rsqrt(jnp.mean(o_t * o_t, axis=0, keepdims=True) + EPS)
                rows = pl.ds(pl.multiple_of(qi * tq + a * Q_SUB, Q_SUB), Q_SUB)
                o_ref[0, rows, cols] = (o_t.T * scale).astype(BF16)
        return nxt

    lax.fori_loop(0, n_tiles, tile, qk(query_maps(0), 0, True))


def _attn_prompt(lams, g_subln, qb, kb, vt, *, tq, heads, li):
    B, T, _ = qb.shape
    assert tq == KV_BLOCK and tq % Q_SUB == 0
    key_chunk = jnp.arange(KV_BLOCK)[None, :, None] // CHUNK
    query = jnp.arange(tq // Q_SUB)[:, None, None] * Q_SUB + jnp.arange(2 * Q_SUB)[None, None, :] % Q_SUB
    bias = jnp.where(key_chunk <= query // CHUNK, 0.0, NEG_INF).astype(F32)
    c2 = lambda b, h: (0, 0)
    small = pl.BlockSpec((1, HEAD_DIM), c2)
    width = heads * HEAD_WIDTH
    chains = heads * (tq // Q_SUB)
    head = pl.BlockSpec((1, T, width), lambda b, h: (b, 0, h))
    return pl.pallas_call(
        functools.partial(_attn_prompt_kernel, tq=tq, heads=heads, li=li),
        grid=(B, N_HEADS // heads),
        in_specs=[small, small, small, small,
                  pl.BlockSpec((1, HEAD_WIDTH), c2),
                  pl.BlockSpec(bias.shape, lambda b, h: (0, 0, 0)),
                  head, head,
                  pl.BlockSpec((1, T // KV_BLOCK, width, KV_BLOCK), lambda b, h: (b, 0, h, 0))],
        out_specs=head,
        out_shape=jax.ShapeDtypeStruct((B, T, QK_WIDTH), BF16),
        scratch_shapes=[pltpu.VMEM((chains, 1, 2 * Q_SUB), F32),
                        pltpu.VMEM((chains, 1, 2 * Q_SUB), F32),
                        pltpu.VMEM((chains, HEAD_WIDTH, 2 * Q_SUB), F32)],
        compiler_params=pltpu.CompilerParams(
            dimension_semantics=("parallel", "parallel"), vmem_limit_bytes=VMEM_LIMIT),
        name="attn_prompt",
    )(*lams, g_subln, bias, qb, kb, vt)


def _attn_sample_kernel(lq1_ref, lk1_ref, lq2_ref, lk2_ref, gsub_ref, q_ref, kc_hbm, vc_hbm,
                        kn_ref, vn_ref, o_ref, kc_buf, vc_buf, sem_ref, *, li):
    T = q_ref.shape[1]
    b = pl.program_id(0)
    slot = lax.rem(b, 2)

    def copies(stream, slot_):
        out = []
        for hd in range(N_HEADS):
            out.append(pltpu.make_async_copy(kc_hbm.at[stream, :, hd, :], kc_buf.at[slot_, hd], sem_ref.at[slot_]))
            out.append(pltpu.make_async_copy(vc_hbm.at[stream, :, hd, :], vc_buf.at[slot_, hd], sem_ref.at[slot_]))
        return out

    @pl.when(b == 0)
    def _():
        for c in copies(0, 0):
            c.start()

    @pl.when(b + 1 < pl.num_programs(0))
    def _():
        for c in copies(b + 1, 1 - slot):
            c.start()

    for c in copies(b, slot):
        c.wait()

    lam = _lambda_value(lq1_ref, lk1_ref, lq2_ref, lk2_ref, li)
    for hd in range(N_HEADS):
        cols = slice(hd * HEAD_WIDTH, (hd + 1) * HEAD_WIDTH)
        q2 = jnp.concatenate(_split_halves(q_ref[0, :, cols]), axis=0)
        kc = kc_buf[slot, hd].astype(BF16)
        vc = vc_buf[slot, hd].astype(BF16)
        sc = _scores(q2, kc)
        sn = _scores(q2, kn_ref[0, :, cols])
        m = jnp.maximum(jnp.max(sc, axis=-1, keepdims=True), jnp.max(sn, axis=-1, keepdims=True))
        pc = jnp.exp(sc - m)
        pn = jnp.exp(sn - m)
        l = jnp.sum(pc, axis=-1, keepdims=True) + jnp.sum(pn, axis=-1, keepdims=True)
        acc = (jnp.dot(pc.astype(BF16), vc, preferred_element_type=F32)
               + jnp.dot(pn.astype(BF16), vn_ref[0, :, cols], preferred_element_type=F32))
        o2 = acc / l
        o = o2[:T] - lam * o2[T:]
        o_ref[0, :, cols] = _subln(o, gsub_ref, li).astype(BF16)


def _attn_sample(lams, g_subln, qb, cache_k, cache_v, kb, vb, *, li):
    B, T, _ = qb.shape
    P = cache_k.shape[1]
    c2 = lambda b: (0, 0)
    small = pl.BlockSpec((1, HEAD_DIM), c2)
    rows = pl.BlockSpec((1, T, QK_WIDTH), lambda b: (b, 0, 0))
    in_hbm = pl.BlockSpec(memory_space=pl.ANY)
    return pl.pallas_call(
        functools.partial(_attn_sample_kernel, li=li),
        grid=(B,),
        in_specs=[small, small, small, small, pl.BlockSpec((1, HEAD_WIDTH), c2),
                  rows, in_hbm, in_hbm, rows, rows],
        out_specs=rows,
        out_shape=jax.ShapeDtypeStruct((B, T, QK_WIDTH), BF16),
        scratch_shapes=[pltpu.VMEM((2, N_HEADS, P, HEAD_WIDTH), F32),
                        pltpu.VMEM((2, N_HEADS, P, HEAD_WIDTH), F32),
                        pltpu.SemaphoreType.DMA((2,))],
        compiler_params=pltpu.CompilerParams(
            dimension_semantics=("arbitrary",), vmem_limit_bytes=VMEM_LIMIT),
        name="attn_sample",
    )(*lams, g_subln, qb, cache_k, cache_v, kb, vb)


def _bf16_split(x):
    hi = x.astype(BF16)
    return hi, (x - hi.astype(F32)).astype(BF16)


def _route_sort_kernel(x_ref, pool_ref, attn_ref, wout_ref, gffn_ref, wrt_ref, brt_ref,
                       x1_ref, route_ref, seg_ref, hs_ref,
                       stage_ref, zero_ref, run_ref, sem_ref, *, tm, sub, cap, nsteps):
    i = pl.program_id(0)
    slot = lax.rem(i, 2)
    slots = tm + (tm // sub) * N_GROUPS * SEG_ALIGN

    @pl.when(i == 0)
    def _():
        for g in range(N_GROUPS):
            run_ref[g] = 0
        stage_ref[...] = jnp.zeros(stage_ref.shape, BF16)
        zero_ref[...] = jnp.zeros(zero_ref.shape, BF16)

    mix = (jnp.dot(pool_ref[...], wout_ref[0:POOL_WIDTH, :], preferred_element_type=F32)
           + jnp.dot(attn_ref[...], wout_ref[POOL_WIDTH:, :], preferred_element_type=F32))
    x1 = x_ref[...] + mix
    x1_ref[...] = x1
    ms = jnp.mean(x1 * x1, axis=-1, keepdims=True)
    h2 = x1 * lax.rsqrt(ms + EPS) * gffn_ref[...]
    h_hi, h_lo = _bf16_split(h2)
    nt = (((1,), (1,)), ((), ()))
    hi_lo = lax.dot_general(wrt_ref[...], h_hi, nt, preferred_element_type=F32)
    logits = (hi_lo[:LANES] + hi_lo[LANES:]
              + lax.dot_general(wrt_ref[0:LANES, :], h_lo, nt, preferred_element_type=F32)
              + brt_ref[...])
    row = lambda r: logits[r:r + 1, :]

    def first_argmax(vals):
        best = functools.reduce(jnp.maximum, vals)
        idx = jnp.full(best.shape, len(vals) - 1, jnp.int32)
        for k in range(len(vals) - 2, -1, -1):
            idx = jnp.where(vals[k] == best, k, idx)
        return best, idx

    group_logits = [row(g) for g in range(N_GROUPS)]
    gmax, g_idx = first_argmax(group_logits)
    g_gate = 1.0 / functools.reduce(jnp.add, [jnp.exp(v - gmax) for v in group_logits])
    in_group = [g_idx == g for g in range(N_GROUPS)]
    e_sel = []
    for k in range(EXPERTS_PER_GROUP):
        v = row(ROUTER_LANE0 + (N_GROUPS - 1) * EXPERTS_PER_GROUP + k)
        for g in range(N_GROUPS - 2, -1, -1):
            v = jnp.where(in_group[g], row(ROUTER_LANE0 + g * EXPERTS_PER_GROUP + k), v)
        e_sel.append(v)
    v1, i1 = first_argmax(e_sel)
    v2, i2 = first_argmax([jnp.where(i1 == k, -jnp.inf, e_sel[k]) for k in range(EXPERTS_PER_GROUP)])
    e21 = jnp.exp(v2 - v1)
    w1 = g_gate / (1.0 + e21)
    w2 = w1 * e21
    w1h = w1.astype(BF16).astype(F32)
    w2h = w2.astype(BF16).astype(F32)

    onehot = [jnp.where(m, 1.0, 0.0) for m in in_group]
    onehot8 = jnp.concatenate(onehot + [jnp.zeros((8 - N_GROUPS, tm), F32)], axis=0)
    earlier = lax.broadcasted_iota(jnp.int32, (tm, tm), 0)
    token = lax.broadcasted_iota(jnp.int32, (tm, tm), 1)
    before = (earlier < token) & (earlier // sub == token // sub)
    rank_all = jnp.dot(onehot8.astype(BF16), jnp.where(before, 1.0, 0.0).astype(BF16),
                       preferred_element_type=F32)
    rank = functools.reduce(jnp.add, [onehot[g] * rank_all[g:g + 1] for g in range(N_GROUPS)])
    n_sub = tm // sub
    n_pad = []
    for s in range(n_sub):
        n = jnp.sum(onehot8[:, s * sub:(s + 1) * sub], axis=1, keepdims=True)
        n_pad.append(jnp.floor((n + (SEG_ALIGN - 1)) * (1.0 / SEG_ALIGN)) * SEG_ALIGN)
    sub_of_token = lax.broadcasted_iota(jnp.int32, (1, tm), 1) // sub
    off = {}
    acc = jnp.zeros((1, 1), F32)
    local = rank
    for g in range(N_GROUPS):
        for s in range(n_sub):
            off[g, s] = acc
            local = local + jnp.where(in_group[g] & (sub_of_token == s), acc, 0.0)
            acc = acc + n_pad[s][g:g + 1]

    hi = [jnp.where(i1 == k, w1h, 0.0) + jnp.where(i2 == k, w2h, 0.0) for k in range(EXPERTS_PER_GROUP)]
    lo = [jnp.where(i1 == k, w1 - w1h, 0.0) + jnp.where(i2 == k, w2 - w2h, 0.0)
          for k in range(EXPERTS_PER_GROUP)]
    where = g_idx.astype(F32) * sub + rank
    per_token = jnp.concatenate(hi + lo + [where, jnp.zeros((LANES - ROUTE_LANE - 1, tm), F32)], axis=0).T
    route_ref[...] = per_token

    place = jnp.where(lax.broadcasted_iota(jnp.int32, (slots, tm), 0).astype(F32) == local, 1.0, 0.0)
    rows = jnp.concatenate([h_hi, per_token.astype(BF16)], axis=1)
    sorted_rows = jnp.dot(place.astype(BF16), rows, preferred_element_type=F32)
    stage_ref[slot, 0:slots, :] = sorted_rows.astype(BF16)

    def window(g, slot_, src_row, dst_row):
        return pltpu.make_async_copy(
            stage_ref.at[slot_, pl.ds(pl.multiple_of(src_row, SEG_ALIGN), tm)],
            hs_ref.at[pl.ds(pl.multiple_of(dst_row, SEG_ALIGN), tm)],
            sem_ref.at[slot_])

    @pl.when(i > 0)
    def _():
        for g in range(N_GROUPS):
            window(g, 1 - slot, 0, 0).wait()

    for g in range(N_GROUPS):
        src = off[g, 0][0, 0].astype(jnp.int32)
        start = run_ref[g]
        window(g, slot, src, g * cap + start).start()
        for s in range(n_sub):
            seg_ref[g, n_sub * i + s] = start
            start = start + n_pad[s][g, 0].astype(jnp.int32)
        run_ref[g] = start
    for s in range(n_sub):
        biggest = jnp.max(n_pad[s][0:N_GROUPS], axis=0, keepdims=True)[0, 0]
        seg_ref[N_GROUPS, n_sub * i + s] = (biggest <= sub // 2).astype(jnp.int32)

    @pl.when(i == nsteps - 1)
    def _():
        for g in range(N_GROUPS):
            window(g, slot, 0, 0).wait()
        fills = []
        for g in range(N_GROUPS):
            total = run_ref[g]
            seg_ref[g, n_sub * nsteps] = total
            for z in range(ZERO_FILL_ROWS // ZERO_ROWS):
                dst = g * cap + total + z * ZERO_ROWS
                fills.append(pltpu.make_async_copy(
                    zero_ref, hs_ref.at[pl.ds(pl.multiple_of(dst, SEG_ALIGN), ZERO_ROWS)],
                    sem_ref.at[2]))
        for f in fills:
            f.start()
        for f in fills:
            f.wait()


def _round_up(x, m):
    return (x + m - 1) // m * m


def _sorted_capacity(n_tokens, ntiles):
    return _round_up(n_tokens + SEG_ALIGN * ntiles, MOE_BLOCK) + 3 * MOE_BLOCK


def _route_sort(x2d, pool2d, attn2d, w_out, g_ffn, wrt, brt, *, tm, sub):
    N = x2d.shape[0]
    nsteps = N // tm
    ntiles = N // sub
    cap = _sorted_capacity(N, ntiles)
    row = lambda i: (i, 0)
    c2 = lambda i: (0, 0)
    return pl.pallas_call(
        functools.partial(_route_sort_kernel, tm=tm, sub=sub, cap=cap, nsteps=nsteps),
        grid=(nsteps,),
        in_specs=[pl.BlockSpec((tm, D_MODEL), row),
                  pl.BlockSpec((tm, POOL_WIDTH), row),
                  pl.BlockSpec((tm, QK_WIDTH), row),
                  pl.BlockSpec((D_MODEL, D_MODEL), c2),
                  pl.BlockSpec((1, D_MODEL), c2),
                  pl.BlockSpec((2 * LANES, D_MODEL), c2),
                  pl.BlockSpec((LANES, 1), c2)],
        out_specs=[pl.BlockSpec((tm, D_MODEL), row),
                   pl.BlockSpec((tm, LANES), row),
                   pl.BlockSpec(memory_space=pltpu.SMEM),
                   pl.BlockSpec(memory_space=pl.ANY)],
        out_shape=[jax.ShapeDtypeStruct((N, D_MODEL), F32),
                   jax.ShapeDtypeStruct((N, LANES), F32),
                   jax.ShapeDtypeStruct((N_GROUPS + 1, ntiles + 1), jnp.int32),
                   jax.ShapeDtypeStruct((N_GROUPS * cap, ROW_W), BF16)],
        scratch_shapes=[pltpu.VMEM((2, 2 * tm + (tm // sub) * N_GROUPS * SEG_ALIGN, ROW_W), BF16),
                        pltpu.VMEM((ZERO_ROWS, ROW_W), BF16),
                        pltpu.SMEM((N_GROUPS,), jnp.int32),
                        pltpu.SemaphoreType.DMA((3,))],
        compiler_params=pltpu.CompilerParams(
            dimension_semantics=("arbitrary",), vmem_limit_bytes=VMEM_LIMIT),
        name="route_sort",
    )(x2d, pool2d, attn2d, w_out, g_ffn, wrt, brt)


def _group_moe_kernel(blk_ref, grp_ref, valid_ref, hs_ref, wg_ref, wu_ref, wd_ref, out_ref):
    b = pl.program_id(0)

    @pl.when(valid_ref[b] == 1)
    def _():
        rows = hs_ref[...]
        x = rows[:, :D_MODEL]
        cparts = rows[:, D_MODEL:].astype(F32)
        lane = lax.broadcasted_iota(jnp.int32, cparts.shape, 1)
        hes = []
        for e in range(EXPERTS_PER_GROUP):
            a = jnp.dot(x, wg_ref[e], preferred_element_type=F32)
            u = jnp.dot(x, wu_ref[e], preferred_element_type=F32)
            c = jnp.sum(jnp.where((lane == e) | (lane == e + EXPERTS_PER_GROUP), cparts, 0.0),
                        axis=-1, keepdims=True)
            hes.append((a * jax.nn.sigmoid(a) * u * c).astype(BF16))
        he = jnp.concatenate(hes, axis=1)
        out_ref[...] = jnp.dot(he, wd_ref[0], preferred_element_type=F32).astype(BF16)

    @pl.when(valid_ref[b] == 0)
    def _():
        out_ref[...] = jnp.zeros(out_ref.shape, BF16)


def _group_moe(blk, grp, valid, hs, w_gate, w_up, w_down_grouped):
    nb = blk.shape[0]
    bm = MOE_BLOCK
    by_block = lambda b, blk, grp, valid: (blk[b], 0)
    by_group = lambda b, blk, grp, valid: (grp[b], 0, 0)
    grid_spec = pltpu.PrefetchScalarGridSpec(
        num_scalar_prefetch=3,
        grid=(nb,),
        in_specs=[pl.BlockSpec((bm, ROW_W), by_block),
                  pl.BlockSpec((EXPERTS_PER_GROUP, D_MODEL, D_EXPERT), by_group),
                  pl.BlockSpec((EXPERTS_PER_GROUP, D_MODEL, D_EXPERT), by_group),
                  pl.BlockSpec((1, EXPERTS_PER_GROUP * D_EXPERT, D_MODEL), by_group)],
        out_specs=pl.BlockSpec((bm, D_MODEL), by_block),
    )
    return pl.pallas_call(
        _group_moe_kernel,
        grid_spec=grid_spec,
        out_shape=jax.ShapeDtypeStruct((hs.shape[0], D_MODEL), BF16),
        compiler_params=pltpu.CompilerParams(
            dimension_semantics=("arbitrary",), vmem_limit_bytes=VMEM_LIMIT),
        name="group_moe",
    )(blk, grp, valid, hs, w_gate, w_up, w_down_grouped)


def _block_table(seg, n_tokens, ntiles, cap):
    bm = MOE_BLOCK
    nb_max = _round_up(n_tokens + N_GROUPS * SEG_ALIGN * ntiles, bm) // bm + 2 * N_GROUPS
    total = seg[:N_GROUPS, ntiles]
    nb = (total + bm - 1) // bm + 1
    end = jnp.cumsum(nb)
    start = end - nb
    b = jnp.arange(nb_max, dtype=jnp.int32)
    g = jnp.sum((b[:, None] >= end[None, :]).astype(jnp.int32), axis=1)
    valid = g < N_GROUPS
    gc = jnp.minimum(g, N_GROUPS - 1)
    blk = jnp.where(valid, gc * (cap // bm) + b - start[gc], N_GROUPS * cap // bm - 1)
    return blk.astype(jnp.int32), gc.astype(jnp.int32), valid.astype(jnp.int32)


def _unsort_norm_kernel(seg_ref, x1_ref, route_ref, gfin_ref, os_ref, y_ref, win_ref, sem_ref,
                        *, tm, cap, ntiles):
    i = pl.program_id(0)
    slot = lax.rem(i, 2)
    half = tm // 2

    def is_small(tile):
        return seg_ref[N_GROUPS * (ntiles + 1) + tile] == 1

    def windows(tile, slot_, rows):
        out = []
        for g in range(N_GROUPS):
            start = seg_ref[g * (ntiles + 1) + tile]
            out.append(pltpu.make_async_copy(
                os_ref.at[pl.ds(pl.multiple_of(g * cap + start, SEG_ALIGN), rows)],
                win_ref.at[slot_, g, pl.ds(0, rows)], sem_ref.at[slot_]))
        return out

    def for_tile(tile, slot_, action):
        for rows, cond in ((half, is_small(tile)), (tm, jnp.logical_not(is_small(tile)))):
            @pl.when(cond)
            def _():
                for c in windows(tile, slot_, rows):
                    action(c)

    @pl.when(i == 0)
    def _():
        for_tile(0, 0, lambda c: c.start())

    @pl.when(i + 1 < ntiles)
    def _():
        for_tile(i + 1, 1 - slot, lambda c: c.start())

    for_tile(i, slot, lambda c: c.wait())

    where = route_ref[:, ROUTE_LANE:ROUTE_LANE + 1]

    def finish(rows, where_):
        pick = jnp.where(lax.broadcasted_iota(jnp.int32, (tm, N_GROUPS * rows), 1).astype(F32) == where_,
                         1.0, 0.0)
        picked = win_ref[slot, :, 0:rows, :].reshape(N_GROUPS * rows, D_MODEL)
        x2 = x1_ref[...] + jnp.dot(pick.astype(BF16), picked, preferred_element_type=F32)
        ms = jnp.mean(x2 * x2, axis=-1, keepdims=True)
        y_ref[...] = x2 * lax.rsqrt(ms + EPS) * gfin_ref[...]

    @pl.when(is_small(i))
    def _():
        group = jnp.floor(where * (1.0 / tm))
        finish(half, where - group * (tm - half))

    @pl.when(jnp.logical_not(is_small(i)))
    def _():
        finish(tm, where)


def _unsort_norm(seg, x1, route, g_final, out_sorted, *, tm, cap):
    N = x1.shape[0]
    ntiles = N // tm
    grid_spec = pltpu.PrefetchScalarGridSpec(
        num_scalar_prefetch=1,
        grid=(ntiles,),
        in_specs=[pl.BlockSpec((tm, D_MODEL), lambda i, seg: (i, 0)),
                  pl.BlockSpec((tm, LANES), lambda i, seg: (i, 0)),
                  pl.BlockSpec((1, D_MODEL), lambda i, seg: (0, 0)),
                  pl.BlockSpec(memory_space=pl.ANY)],
        out_specs=pl.BlockSpec((tm, D_MODEL), lambda i, seg: (i, 0)),
        scratch_shapes=[pltpu.VMEM((2, N_GROUPS, tm, D_MODEL), BF16),
                        pltpu.SemaphoreType.DMA((2,))],
    )
    return pl.pallas_call(
        functools.partial(_unsort_norm_kernel, tm=tm, cap=cap, ntiles=ntiles),
        grid_spec=grid_spec,
        out_shape=jax.ShapeDtypeStruct((N, D_MODEL), F32),
        compiler_params=pltpu.CompilerParams(
            dimension_semantics=("arbitrary",), vmem_limit_bytes=VMEM_LIMIT),
        name="unsort_norm",
    )(seg.reshape(-1), x1, route, g_final, out_sorted)


def _layer(x, hist16, cache, start_pos, layer, w, g_final, *, tt, tm):
    B, T, _ = x.shape
    li = _lambda_init(layer)
    k, v, qb, kb, vb, pool, state = _inproj(x, hist16, w["g_mix"], w["w_in"], w["w_vt"], w["w_pool"],
                                            w["pool_scale"], tt=tt, start_pos=start_pos,
                                            transposed_v=cache is None)
    if cache is None:
        attn = _attn_prompt(w["lams"], w["g_subln"], qb, kb, vb, tq=KV_BLOCK, heads=ATTN_HEADS_PER_STEP, li=li)
    else:
        attn = _attn_sample(w["lams"], w["g_subln"], qb, cache[0], cache[1], kb, vb, li=li)
    N = B * T
    ntiles = N // tm
    cap = _sorted_capacity(N, ntiles)
    x1, route, seg, hs = _route_sort(x.reshape(N, D_MODEL), pool.reshape(N, POOL_WIDTH),
                                     attn.reshape(N, QK_WIDTH), w["w_out"], w["g_ffn"],
                                     w["wrt"], w["brt"], tm=ROUTE_STEP, sub=tm)
    blk, grp, valid = _block_table(seg, N, ntiles, cap)
    out_sorted = _group_moe(blk, grp, valid, hs, w["w_gate"], w["w_up"], w["w_down"])
    y = _unsort_norm(seg, x1, route, g_final, out_sorted, tm=tm, cap=cap)
    return y.reshape(B, T, D_MODEL), k, v, state


def kernel(x_prompt, x_sample, cache_k, cache_v, state_pool, g_mix, w_in, w_pool, pool_scale, lam_q1, lam_k1, lam_q2, lam_k2, g_subln, w_out, g_ffn, w_group, b_group, w_erouter, b_erouter, w_gate, w_up, w_down, g_final):
    depth = g_mix.shape[0]
    assert depth == 1, "the final norm is fused into the layer, so exactly one layer is supported"
    l = 0
    wr = jnp.concatenate([w_group[l], w_erouter[l]], axis=1).T
    wr = jnp.pad(wr, ((0, LANES - wr.shape[0]), (0, 0)))
    wr_hi = wr.astype(BF16)
    wrt = jnp.concatenate([wr_hi, (wr - wr_hi.astype(F32)).astype(BF16)], axis=0)
    br = jnp.pad(jnp.concatenate([b_group[l], b_erouter[l]]), (0, LANES - N_GROUPS - N_EXPERTS))
    w = dict(
        g_mix=g_mix[l][None], w_in=w_in[l].astype(BF16),
        w_vt=w_in[l][:, POOL_WIDTH + 2 * QK_WIDTH:].T.astype(BF16), w_pool=w_pool[l].astype(BF16),
        pool_scale=pool_scale[l][None],
        lams=(lam_q1[l][None], lam_k1[l][None], lam_q2[l][None], lam_k2[l][None]),
        g_subln=g_subln[l][None], w_out=w_out[l].astype(BF16), g_ffn=g_ffn[l][None],
        wrt=wrt, brt=br[:, None],
        w_gate=w_gate[l].astype(BF16), w_up=w_up[l].astype(BF16),
        w_down=w_down[l].astype(BF16).reshape(N_GROUPS, EXPERTS_PER_GROUP * D_EXPERT, D_MODEL),
    )
    gfin = g_final[None]
    Bp = x_prompt.shape[0]
    Bs, _, _ = x_sample.shape
    P = cache_k.shape[2]
    zero_hist = jnp.zeros((Bp, HIST_ROWS, POOL_WIDTH), F32)
    samp_hist = jnp.pad(state_pool[l], ((0, 0), (HIST_ROWS - POOL_HIST, 0), (0, 0)))
    cache = (cache_k[l], cache_v[l])

    yp, kp, vp, pp = _layer(x_prompt, zero_hist, None, 0, l, w, gfin, tt=512, tm=ROUTE_TILE)
    ys, kn, vn, pn = _layer(x_sample, samp_hist, cache, P, l, w, gfin, tt=64, tm=ROUTE_TILE)
    return (yp, ys, kp[None], vp[None], pp[None], kn[None], vn[None], pn[None])
```

```python
import functools
import math

import jax
import jax.numpy as jnp
from jax import lax
from jax.experimental import pallas as pl
from jax.experimental.pallas import tpu as pltpu

F32 = jnp.float32
BF16 = jnp.bfloat16

D_MODEL = 1024
CHUNK = 64
POOL_WIDTH = 512
POOL_WINDOWS = (2, 4, 8, 16)
POOL_GROUP = 128
POOL_HIST = 15
HIST_ROWS = 16
N_HEADS = 4
HEAD_DIM = 64
HEAD_WIDTH = 128
QK_WIDTH = 512
PROJ_WIDTH = 2048
ATTN_SCALE = HEAD_DIM ** -0.5
LOG2_E = math.log2(math.e)
NEG_INF = -1e30
N_GROUPS = 4
EXPERTS_PER_GROUP = 4
N_EXPERTS = 16
D_EXPERT = 256
EPS = 1e-6
LANES = 128
KV_BLOCK = 256
Q_SUB = 128
ATTN_HEADS_PER_STEP = 2
ROUTER_LANE0 = N_GROUPS
ROUTE_TILE = 256
ROUTE_STEP = 512
ROUTE_LANE = 8
SEG_ALIGN = 16
ROW_W = D_MODEL + LANES
MOE_BLOCK = 512
ZERO_ROWS = 256
ZERO_FILL_ROWS = 2 * MOE_BLOCK
VMEM_LIMIT = 48 * 1024 * 1024


def _lambda_init(layer):
    return 0.8 - 0.6 * math.exp(-0.3 * layer)


def _inproj_kernel(x_ref, hist_ref, gmix_ref, win_ref, wvt_ref, wpool_ref, pscale_ref,
                   k_hbm, v_hbm, qb_ref, kb_ref, vb_ref, pool_ref, state_ref,
                   ext_ref, kv_buf, sem_ref, *, tt, start_pos, transposed_v):
    b = pl.program_id(0)
    t = pl.program_id(1)
    step = b * pl.num_programs(1) + t
    last_step = pl.num_programs(0) * pl.num_programs(1) - 1
    slot = lax.rem(step, 2)

    def kv_copies(bb, tile, slot_):
        out = []
        for which, dst in enumerate((k_hbm, v_hbm)):
            for hd in range(N_HEADS):
                out.append(pltpu.make_async_copy(
                    kv_buf.at[slot_, which, :, pl.ds(hd * HEAD_WIDTH, HEAD_WIDTH)],
                    dst.at[bb, pl.ds(pl.multiple_of(tile * tt, tt), tt), hd, :],
                    sem_ref.at[slot_]))
        return out

    x = x_ref[0]
    ms = jnp.mean(x * x, axis=-1, keepdims=True)
    h = (x * lax.rsqrt(ms + EPS) * gmix_ref[...]).astype(BF16)
    proj = jnp.dot(h, win_ref[...], preferred_element_type=F32)
    u = proj[:, :POOL_WIDTH]
    q = proj[:, POOL_WIDTH:POOL_WIDTH + QK_WIDTH]
    k = proj[:, POOL_WIDTH + QK_WIDTH:POOL_WIDTH + 2 * QK_WIDTH]
    v = proj[:, POOL_WIDTH + 2 * QK_WIDTH:]

    @pl.when(step >= 2)
    def _():
        for c in kv_copies(b, t, slot):
            c.wait()

    kv_buf[slot, 0] = k
    kv_buf[slot, 1] = v
    for c in kv_copies(b, t, slot):
        c.start()

    @pl.when(step == last_step)
    def _():
        for c in kv_copies(b, t, 1 - slot) + kv_copies(b, t, slot):
            c.wait()

    qb_ref[0] = (q * (ATTN_SCALE * LOG2_E if transposed_v else ATTN_SCALE)).astype(BF16)
    kb_ref[0] = k.astype(BF16)
    if transposed_v:
        vt = lax.dot_general(wvt_ref[...], h, (((1,), (1,)), ((), ())),
                             preferred_element_type=F32).astype(BF16)
        for i in range(tt // KV_BLOCK):
            vb_ref[0, i] = vt[:, i * KV_BLOCK:(i + 1) * KV_BLOCK]
    else:
        vb_ref[0] = v.astype(BF16)

    @pl.when(t == 0)
    def _():
        ext_ref[0:HIST_ROWS] = hist_ref[0]

    @pl.when(t > 0)
    def _():
        ext_ref[0:HIST_ROWS] = ext_ref[tt:tt + HIST_ROWS]

    ext_ref[HIST_ROWS:HIST_ROWS + tt] = u

    pos = start_pos + t * tt + lax.broadcasted_iota(jnp.int32, (tt, 1), 0)
    outs = []
    for g, w in enumerate(POOL_WINDOWS):
        lo = g * POOL_GROUP
        ug = u[:, lo:lo + POOL_GROUP]
        acc = ug
        for j in range(1, w):
            acc = acc + ext_ref[HIST_ROWS - j:HIST_ROWS - j + tt, lo:lo + POOL_GROUP]
        inv_cnt = 1.0 / jnp.minimum(w, pos + 1).astype(F32)
        d = acc * inv_cnt - ug
        outs.append(jnp.dot(d.astype(BF16), wpool_ref[g], preferred_element_type=F32))
    y = jnp.concatenate(outs, axis=-1) * pscale_ref[...]
    pool_ref[0] = y.astype(BF16)

    @pl.when(t == pl.num_programs(1) - 1)
    def _():
        state_ref[0] = ext_ref[tt + 1:tt + HIST_ROWS]


def _inproj(x, hist16, g_mix, w_in, w_vt, w_pool, pool_scale, *, tt, start_pos, transposed_v):
    B, T, _ = x.shape
    nt = T // tt
    row = lambda b, t: (b, t, 0)
    const2 = lambda b, t: (0, 0)
    wide = lambda dt: jax.ShapeDtypeStruct((B, T, QK_WIDTH), dt)
    if transposed_v:
        vb_spec = pl.BlockSpec((1, tt // KV_BLOCK, QK_WIDTH, KV_BLOCK), lambda b, t: (b, t, 0, 0))
        vb_shape = jax.ShapeDtypeStruct((B, T // KV_BLOCK, QK_WIDTH, KV_BLOCK), BF16)
    else:
        vb_spec = pl.BlockSpec((1, tt, QK_WIDTH), row)
        vb_shape = wide(BF16)
    return pl.pallas_call(
        functools.partial(_inproj_kernel, tt=tt, start_pos=start_pos, transposed_v=transposed_v),
        grid=(B, nt),
        in_specs=[
            pl.BlockSpec((1, tt, D_MODEL), row),
            pl.BlockSpec((1, HIST_ROWS, POOL_WIDTH), lambda b, t: (b, 0, 0)),
            pl.BlockSpec((1, D_MODEL), const2),
            pl.BlockSpec((D_MODEL, PROJ_WIDTH), const2),
            pl.BlockSpec((QK_WIDTH, D_MODEL), const2),
            pl.BlockSpec((len(POOL_WINDOWS), POOL_GROUP, POOL_GROUP), lambda b, t: (0, 0, 0)),
            pl.BlockSpec((1, POOL_WIDTH), const2),
        ],
        out_specs=[
            pl.BlockSpec(memory_space=pl.ANY),
            pl.BlockSpec(memory_space=pl.ANY),
            pl.BlockSpec((1, tt, QK_WIDTH), row),
            pl.BlockSpec((1, tt, QK_WIDTH), row),
            vb_spec,
            pl.BlockSpec((1, tt, POOL_WIDTH), row),
            pl.BlockSpec((1, POOL_HIST, POOL_WIDTH), lambda b, t: (b, 0, 0)),
        ],
        out_shape=[jax.ShapeDtypeStruct((B, T, N_HEADS, HEAD_WIDTH), F32),
                   jax.ShapeDtypeStruct((B, T, N_HEADS, HEAD_WIDTH), F32),
                   wide(BF16), wide(BF16), vb_shape, wide(BF16),
                   jax.ShapeDtypeStruct((B, POOL_HIST, POOL_WIDTH), F32)],
        scratch_shapes=[pltpu.VMEM((HIST_ROWS + tt, POOL_WIDTH), F32),
                        pltpu.VMEM((2, 2, tt, QK_WIDTH), F32),
                        pltpu.SemaphoreType.DMA((2,))],
        compiler_params=pltpu.CompilerParams(
            dimension_semantics=("arbitrary", "arbitrary"), vmem_limit_bytes=VMEM_LIMIT),
        name="inproj_pool",
    )(x, hist16, g_mix, w_in, w_vt, w_pool, pool_scale)


def _lambda_value(lq1_ref, lk1_ref, lq2_ref, lk2_ref, li):
    s1 = jnp.sum(lq1_ref[...] * lk1_ref[...], axis=-1, keepdims=True)
    s2 = jnp.sum(lq2_ref[...] * lk2_ref[...], axis=-1, keepdims=True)
    return jnp.exp(s1) - jnp.exp(s2) + li


def _split_halves(q):
    lane = lax.broadcasted_iota(jnp.int32, (1, HEAD_WIDTH), 1)
    zero = jnp.zeros_like(q)
    return jnp.where(lane < HEAD_DIM, q, zero), jnp.where(lane >= HEAD_DIM, q, zero)


def _scores(qh, kblk):
    return lax.dot_general(qh, kblk, (((1,), (1,)), ((), ())), preferred_element_type=F32)


def _subln(o, gsub_ref, li):
    return o * lax.rsqrt(jnp.mean(o * o, axis=-1, keepdims=True) + EPS) * gsub_ref[...] * (1.0 - li)


def _attn_prompt_kernel(lq1_ref, lk1_ref, lq2_ref, lk2_ref, gsub_ref, bias_ref, q_ref, k_ref, vt_ref, o_ref,
                        m_ref, l_ref, acc_ref, *, tq, heads, li):
    tk = KV_BLOCK
    n_sub = tq // Q_SUB
    n_tiles = q_ref.shape[1] // tq
    head_cols = [slice(hh * HEAD_WIDTH, (hh + 1) * HEAD_WIDTH) for hh in range(heads)]

    def query_maps(qi):
        q2 = []
        for cols in head_cols:
            for a in range(n_sub):
                rows = q_ref[0, pl.ds(pl.multiple_of(qi * tq + a * Q_SUB, Q_SUB), Q_SUB), cols]
                q0, q1 = _split_halves(rows)
                q2.append(jnp.concatenate([q0, q1], axis=0))
        return q2

    def qk(q2, j, masked):
        out = []
        for hh, cols in enumerate(head_cols):
            kblk = k_ref[0, pl.ds(pl.multiple_of(j * tk, tk), tk), cols]
            for a in range(n_sub):
                s = _scores(kblk, q2[hh * n_sub + a])
                if masked:
                    s = s + bias_ref[a]
                out.append((s, jnp.max(s, axis=0, keepdims=True)))
        return tuple(out)

    ones_rows = jnp.ones((SEG_ALIGN, tk), BF16)

    def consume(j, sc):
        for hh, cols in enumerate(head_cols):
            vtb = jnp.concatenate([vt_ref[0, j, cols, :], ones_rows], axis=0)
            for a in range(n_sub):
                c = hh * n_sub + a
                s, cmax = sc[c]
                m_old = m_ref[c]
                m_new = jnp.maximum(m_old, cmax)
                alpha = jnp.exp2(m_old - m_new)
                p = jnp.exp2(s - m_new)
                pv = jnp.dot(vtb, p.astype(BF16), preferred_element_type=F32)
                l_ref[c] = alpha * l_ref[c] + pv[HEAD_WIDTH:HEAD_WIDTH + 1]
                acc_ref[c] = alpha * acc_ref[c] + pv[:HEAD_WIDTH]
                m_ref[c] = m_new

    lam = _lambda_value(lq1_ref, lk1_ref, lq2_ref, lk2_ref, li)
    scale = gsub_ref[...] * (1.0 - li)

    def tile(qi, sc):
        q2 = query_maps(qi)
        m_ref[...] = jnp.full(m_ref.shape, NEG_INF, F32)
        l_ref[...] = jnp.zeros(l_ref.shape, F32)
        acc_ref[...] = jnp.zeros(acc_ref.shape, F32)

        def body(j, carry):
            prev, cur = carry
            nxt = qk(q2, j, False)
            consume(prev, cur)
            return j, nxt

        last, cur = lax.fori_loop(0, qi, body, (qi, sc))
        qn = jnp.minimum(qi + 1, n_tiles - 1)
        nxt = qk(query_maps(qn), qn, True)
        consume(last, cur)
        for hh, cols in enumerate(head_cols):
            for a in range(n_sub):
                c = hh * n_sub + a
                o2 = acc_ref[c] / l_ref[c]
                o_t = o2[:, :Q_SUB] - lam * o2[:, Q_SUB:]
                o_t = o_t * lax.rsqrt(jnp.mean(o_t * o_t, axis=0, keepdims=True) + EPS)
                rows = pl.ds(pl.multiple_of(qi * tq + a * Q_SUB, Q_SUB), Q_SUB)
                o_ref[0, rows, cols] = (o_t.T * scale).astype(BF16)
        return nxt

    lax.fori_loop(0, n_tiles, tile, qk(query_maps(0), 0, True))


def _attn_prompt(lams, g_subln, qb, kb, vt, *, tq, heads, li):
    B, T, _ = qb.shape
    assert tq == KV_BLOCK and tq % Q_SUB == 0
    key_chunk = jnp.arange(KV_BLOCK)[None, :, None] // CHUNK
    query = jnp.arange(tq // Q_SUB)[:, None, None] * Q_SUB + jnp.arange(2 * Q_SUB)[None, None, :] % Q_SUB
    bias = jnp.where(key_chunk <= query // CHUNK, 0.0, NEG_INF).astype(F32)
    c2 = lambda b, h: (0, 0)
    small = pl.BlockSpec((1, HEAD_DIM), c2)
    width = heads * HEAD_WIDTH
    chains = heads * (tq // Q_SUB)
    head = pl.BlockSpec((1, T, width), lambda b, h: (b, 0, h))
    return pl.pallas_call(
        functools.partial(_attn_prompt_kernel, tq=tq, heads=heads, li=li),
        grid=(B, N_HEADS // heads),
        in_specs=[small, small, small, small,
                  pl.BlockSpec((1, HEAD_WIDTH), c2),
                  pl.BlockSpec(bias.shape, lambda b, h: (0, 0, 0)),
                  head, head,
                  pl.BlockSpec((1, T // KV_BLOCK, width, KV_BLOCK), lambda b, h: (b, 0, h, 0))],
        out_specs=head,
        out_shape=jax.ShapeDtypeStruct((B, T, QK_WIDTH), BF16),
        scratch_shapes=[pltpu.VMEM((chains, 1, 2 * Q_SUB), F32),
                        pltpu.VMEM((chains, 1, 2 * Q_SUB), F32),
                        pltpu.VMEM((chains, HEAD_WIDTH, 2 * Q_SUB), F32)],
        compiler_params=pltpu.CompilerParams(
            dimension_semantics=("parallel", "parallel"), vmem_limit_bytes=VMEM_LIMIT),
        name="attn_prompt",
    )(*lams, g_subln, bias, qb, kb, vt)


def _attn_sample_kernel(lq1_ref, lk1_ref, lq2_ref, lk2_ref, gsub_ref, q_ref, kc_hbm, vc_hbm,
                        kn_ref, vn_ref, o_ref, kc_buf, vc_buf, sem_ref, *, li):
    T = q_ref.shape[1]
    b = pl.program_id(0)
    slot = lax.rem(b, 2)

    def copies(stream, slot_):
        out = []
        for hd in range(N_HEADS):
            out.append(pltpu.make_async_copy(kc_hbm.at[stream, :, hd, :], kc_buf.at[slot_, hd], sem_ref.at[slot_]))
            out.append(pltpu.make_async_copy(vc_hbm.at[stream, :, hd, :], vc_buf.at[slot_, hd], sem_ref.at[slot_]))
        return out

    @pl.when(b == 0)
    def _():
        for c in copies(0, 0):
            c.start()

    @pl.when(b + 1 < pl.num_programs(0))
    def _():
        for c in copies(b + 1, 1 - slot):
            c.start()

    for c in copies(b, slot):
        c.wait()

    lam = _lambda_value(lq1_ref, lk1_ref, lq2_ref, lk2_ref, li)
    for hd in range(N_HEADS):
        cols = slice(hd * HEAD_WIDTH, (hd + 1) * HEAD_WIDTH)
        q2 = jnp.concatenate(_split_halves(q_ref[0, :, cols]), axis=0)
        kc = kc_buf[slot, hd].astype(BF16)
        vc = vc_buf[slot, hd].astype(BF16)
        sc = _scores(q2, kc)
        sn = _scores(q2, kn_ref[0, :, cols])
        m = jnp.maximum(jnp.max(sc, axis=-1, keepdims=True), jnp.max(sn, axis=-1, keepdims=True))
        pc = jnp.exp(sc - m)
        pn = jnp.exp(sn - m)
        l = jnp.sum(pc, axis=-1, keepdims=True) + jnp.sum(pn, axis=-1, keepdims=True)
        acc = (jnp.dot(pc.astype(BF16), vc, preferred_element_type=F32)
               + jnp.dot(pn.astype(BF16), vn_ref[0, :, cols], preferred_element_type=F32))
        o2 = acc / l
        o = o2[:T] - lam * o2[T:]
        o_ref[0, :, cols] = _subln(o, gsub_ref, li).astype(BF16)


def _attn_sample(lams, g_subln, qb, cache_k, cache_v, kb, vb, *, li):
    B, T, _ = qb.shape
    P = cache_k.shape[1]
    c2 = lambda b: (0, 0)
    small = pl.BlockSpec((1, HEAD_DIM), c2)
    rows = pl.BlockSpec((1, T, QK_WIDTH), lambda b: (b, 0, 0))
    in_hbm = pl.BlockSpec(memory_space=pl.ANY)
    return pl.pallas_call(
        functools.partial(_attn_sample_kernel, li=li),
        grid=(B,),
        in_specs=[small, small, small, small, pl.BlockSpec((1, HEAD_WIDTH), c2),
                  rows, in_hbm, in_hbm, rows, rows],
        out_specs=rows,
        out_shape=jax.ShapeDtypeStruct((B, T, QK_WIDTH), BF16),
        scratch_shapes=[pltpu.VMEM((2, N_HEADS, P, HEAD_WIDTH), F32),
                        pltpu.VMEM((2, N_HEADS, P, HEAD_WIDTH), F32),
                        pltpu.SemaphoreType.DMA((2,))],
        compiler_params=pltpu.CompilerParams(
            dimension_semantics=("arbitrary",), vmem_limit_bytes=VMEM_LIMIT),
        name="attn_sample",
    )(*lams, g_subln, qb, cache_k, cache_v, kb, vb)


def _bf16_split(x):
    hi = x.astype(BF16)
    return hi, (x - hi.astype(F32)).astype(BF16)


def _route_sort_kernel(x_ref, pool_ref, attn_ref, wout_ref, gffn_ref, wrt_ref, brt_ref,
                       x1_ref, route_ref, seg_ref, hs_ref,
                       stage_ref, zero_ref, run_ref, sem_ref, *, tm, sub, cap, nsteps):
    i = pl.program_id(0)
    slot = lax.rem(i, 2)
    slots = tm + (tm // sub) * N_GROUPS * SEG_ALIGN

    @pl.when(i == 0)
    def _():
        for g in range(N_GROUPS):
            run_ref[g] = 0
        stage_ref[...] = jnp.zeros(stage_ref.shape, BF16)
        zero_ref[...] = jnp.zeros(zero_ref.shape, BF16)

    mix = (jnp.dot(pool_ref[...], wout_ref[0:POOL_WIDTH, :], preferred_element_type=F32)
           + jnp.dot(attn_ref[...], wout_ref[POOL_WIDTH:, :], preferred_element_type=F32))
    x1 = x_ref[...] + mix
    x1_ref[...] = x1
    ms = jnp.mean(x1 * x1, axis=-1, keepdims=True)
    h2 = x1 * lax.rsqrt(ms + EPS) * gffn_ref[...]
    h_hi, h_lo = _bf16_split(h2)
    nt = (((1,), (1,)), ((), ()))
    hi_lo = lax.dot_general(wrt_ref[...], h_hi, nt, preferred_element_type=F32)
    logits = (hi_lo[:LANES] + hi_lo[LANES:]
              + lax.dot_general(wrt_ref[0:LANES, :], h_lo, nt, preferred_element_type=F32)
              + brt_ref[...])
    row = lambda r: logits[r:r + 1, :]

    def first_argmax(vals):
        best = functools.reduce(jnp.maximum, vals)
        idx = jnp.full(best.shape, len(vals) - 1, jnp.int32)
        for k in range(len(vals) - 2, -1, -1):
            idx = jnp.where(vals[k] == best, k, idx)
        return best, idx

    group_logits = [row(g) for g in range(N_GROUPS)]
    gmax, g_idx = first_argmax(group_logits)
    g_gate = 1.0 / functools.reduce(jnp.add, [jnp.exp(v - gmax) for v in group_logits])
    in_group = [g_idx == g for g in range(N_GROUPS)]
    e_sel = []
    for k in range(EXPERTS_PER_GROUP):
        v = row(ROUTER_LANE0 + (N_GROUPS - 1) * EXPERTS_PER_GROUP + k)
        for g in range(N_GROUPS - 2, -1, -1):
            v = jnp.where(in_group[g], row(ROUTER_LANE0 + g * EXPERTS_PER_GROUP + k), v)
        e_sel.append(v)
    v1, i1 = first_argmax(e_sel)
    v2, i2 = first_argmax([jnp.where(i1 == k, -jnp.inf, e_sel[k]) for k in range(EXPERTS_PER_GROUP)])
    e21 = jnp.exp(v2 - v1)
    w1 = g_gate / (1.0 + e21)
    w2 = w1 * e21
    w1h = w1.astype(BF16).astype(F32)
    w2h = w2.astype(BF16).astype(F32)

    onehot = [jnp.where(m, 1.0, 0.0) for m in in_group]
    onehot8 = jnp.concatenate(onehot + [jnp.zeros((8 - N_GROUPS, tm), F32)], axis=0)
    earlier = lax.broadcasted_iota(jnp.int32, (tm, tm), 0)
    token = lax.broadcasted_iota(jnp.int32, (tm, tm), 1)
    before = (earlier < token) & (earlier // sub == token // sub)
    rank_all = jnp.dot(onehot8.astype(BF16), jnp.where(before, 1.0, 0.0).astype(BF16),
                       preferred_element_type=F32)
    rank = functools.reduce(jnp.add, [onehot[g] * rank_all[g:g + 1] for g in range(N_GROUPS)])
    n_sub = tm // sub
    n_pad = []
    for s in range(n_sub):
        n = jnp.sum(onehot8[:, s * sub:(s + 1) * sub], axis=1, keepdims=True)
        n_pad.append(jnp.floor((n + (SEG_ALIGN - 1)) * (1.0 / SEG_ALIGN)) * SEG_ALIGN)
    sub_of_token = lax.broadcasted_iota(jnp.int32, (1, tm), 1) // sub
    off = {}
    acc = jnp.zeros((1, 1), F32)
    local = rank
    for g in range(N_GROUPS):
        for s in range(n_sub):
            off[g, s] = acc
            local = local + jnp.where(in_group[g] & (sub_of_token == s), acc, 0.0)
            acc = acc + n_pad[s][g:g + 1]

    hi = [jnp.where(i1 == k, w1h, 0.0) + jnp.where(i2 == k, w2h, 0.0) for k in range(EXPERTS_PER_GROUP)]
    lo = [jnp.where(i1 == k, w1 - w1h, 0.0) + jnp.where(i2 == k, w2 - w2h, 0.0)
          for k in range(EXPERTS_PER_GROUP)]
    where = g_idx.astype(F32) * sub + rank
    per_token = jnp.concatenate(hi + lo + [where, jnp.zeros((LANES - ROUTE_LANE - 1, tm), F32)], axis=0).T
    route_ref[...] = per_token

    place = jnp.where(lax.broadcasted_iota(jnp.int32, (slots, tm), 0).astype(F32) == local, 1.0, 0.0)
    rows = jnp.concatenate([h_hi, per_token.astype(BF16)], axis=1)
    sorted_rows = jnp.dot(place.astype(BF16), rows, preferred_element_type=F32)
    stage_ref[slot, 0:slots, :] = sorted_rows.astype(BF16)

    def window(g, slot_, src_row, dst_row):
        return pltpu.make_async_copy(
            stage_ref.at[slot_, pl.ds(pl.multiple_of(src_row, SEG_ALIGN), tm)],
            hs_ref.at[pl.ds(pl.multiple_of(dst_row, SEG_ALIGN), tm)],
            sem_ref.at[slot_])

    @pl.when(i > 0)
    def _():
        for g in range(N_GROUPS):
            window(g, 1 - slot, 0, 0).wait()

    for g in range(N_GROUPS):
        src = off[g, 0][0, 0].astype(jnp.int32)
        start = run_ref[g]
        window(g, slot, src, g * cap + start).start()
        for s in range(n_sub):
            seg_ref[g, n_sub * i + s] = start
            start = start + n_pad[s][g, 0].astype(jnp.int32)
        run_ref[g] = start
    for s in range(n_sub):
        biggest = jnp.max(n_pad[s][0:N_GROUPS], axis=0, keepdims=True)[0, 0]
        seg_ref[N_GROUPS, n_sub * i + s] = (biggest <= sub // 2).astype(jnp.int32)

    @pl.when(i == nsteps - 1)
    def _():
        for g in range(N_GROUPS):
            window(g, slot, 0, 0).wait()
        fills = []
        for g in range(N_GROUPS):
            total = run_ref[g]
            seg_ref[g, n_sub * nsteps] = total
            seg_ref[N_GROUPS, n_sub * nsteps] = 0
            for z in range(ZERO_FILL_ROWS // ZERO_ROWS):
                dst = g * cap + total + z * ZERO_ROWS
                fills.append(pltpu.make_async_copy(
                    zero_ref, hs_ref.at[pl.ds(pl.multiple_of(dst, SEG_ALIGN), ZERO_ROWS)],
                    sem_ref.at[2]))
        for f in fills:
            f.start()
        for f in fills:
            f.wait()


def _round_up(x, m):
    return (x + m - 1) // m * m


def _sorted_capacity(n_tokens, ntiles):
    return _round_up(n_tokens + SEG_ALIGN * ntiles, MOE_BLOCK) + 3 * MOE_BLOCK


def _route_sort(x2d, pool2d, attn2d, w_out, g_ffn, wrt, brt, *, tm, sub):
    N = x2d.shape[0]
    nsteps = N // tm
    ntiles = N // sub
    cap = _sorted_capacity(N, ntiles)
    row = lambda i: (i, 0)
    c2 = lambda i: (0, 0)
    return pl.pallas_call(
        functools.partial(_route_sort_kernel, tm=tm, sub=sub, cap=cap, nsteps=nsteps),
        grid=(nsteps,),
        in_specs=[pl.BlockSpec((tm, D_MODEL), row),
                  pl.BlockSpec((tm, POOL_WIDTH), row),
                  pl.BlockSpec((tm, QK_WIDTH), row),
                  pl.BlockSpec((D_MODEL, D_MODEL), c2),
                  pl.BlockSpec((1, D_MODEL), c2),
                  pl.BlockSpec((2 * LANES, D_MODEL), c2),
                  pl.BlockSpec((LANES, 1), c2)],
        out_specs=[pl.BlockSpec((tm, D_MODEL), row),
                   pl.BlockSpec((tm, LANES), row),
                   pl.BlockSpec(memory_space=pltpu.SMEM),
                   pl.BlockSpec(memory_space=pl.ANY)],
        out_shape=[jax.ShapeDtypeStruct((N, D_MODEL), F32),
                   jax.ShapeDtypeStruct((N, LANES), F32),
                   jax.ShapeDtypeStruct((N_GROUPS + 1, ntiles + 1), jnp.int32),
                   jax.ShapeDtypeStruct((N_GROUPS * cap, ROW_W), BF16)],
        scratch_shapes=[pltpu.VMEM((2, 2 * tm + (tm // sub) * N_GROUPS * SEG_ALIGN, ROW_W), BF16),
                        pltpu.VMEM((ZERO_ROWS, ROW_W), BF16),
                        pltpu.SMEM((N_GROUPS,), jnp.int32),
                        pltpu.SemaphoreType.DMA((3,))],
        compiler_params=pltpu.CompilerParams(
            dimension_semantics=("arbitrary",), vmem_limit_bytes=VMEM_LIMIT),
        name="route_sort",
    )(x2d, pool2d, attn2d, w_out, g_ffn, wrt, brt)


def _group_moe_kernel(blk_ref, grp_ref, valid_ref, hs_ref, wg_ref, wu_ref, wd_ref, out_ref):
    b = pl.program_id(0)

    @pl.when(valid_ref[b] == 1)
    def _():
        rows = hs_ref[...]
        x = rows[:, :D_MODEL]
        cparts = rows[:, D_MODEL:].astype(F32)
        lane = lax.broadcasted_iota(jnp.int32, cparts.shape, 1)
        hes = []
        for e in range(EXPERTS_PER_GROUP):
            a = jnp.dot(x, wg_ref[e], preferred_element_type=F32)
            u = jnp.dot(x, wu_ref[e], preferred_element_type=F32)
            c = jnp.sum(jnp.where((lane == e) | (lane == e + EXPERTS_PER_GROUP), cparts, 0.0),
                        axis=-1, keepdims=True)
            hes.append((a * jax.nn.sigmoid(a) * u * c).astype(BF16))
        he = jnp.concatenate(hes, axis=1)
        out_ref[...] = jnp.dot(he, wd_ref[0], preferred_element_type=F32).astype(BF16)

    @pl.when(valid_ref[b] == 0)
    def _():
        out_ref[...] = jnp.zeros(out_ref.shape, BF16)


def _group_moe(blk, grp, valid, hs, w_gate, w_up, w_down_grouped):
    nb = blk.shape[0]
    bm = MOE_BLOCK
    by_block = lambda b, blk, grp, valid: (blk[b], 0)
    by_group = lambda b, blk, grp, valid: (grp[b], 0, 0)
    grid_spec = pltpu.PrefetchScalarGridSpec(
        num_scalar_prefetch=3,
        grid=(nb,),
        in_specs=[pl.BlockSpec((bm, ROW_W), by_block),
                  pl.BlockSpec((EXPERTS_PER_GROUP, D_MODEL, D_EXPERT), by_group),
                  pl.BlockSpec((EXPERTS_PER_GROUP, D_MODEL, D_EXPERT), by_group),
                  pl.BlockSpec((1, EXPERTS_PER_GROUP * D_EXPERT, D_MODEL), by_group)],
        out_specs=pl.BlockSpec((bm, D_MODEL), by_block),
    )
    return pl.pallas_call(
        _group_moe_kernel,
        grid_spec=grid_spec,
        out_shape=jax.ShapeDtypeStruct((hs.shape[0], D_MODEL), BF16),
        compiler_params=pltpu.CompilerParams(
            dimension_semantics=("arbitrary",), vmem_limit_bytes=VMEM_LIMIT),
        name="group_moe",
    )(blk, grp, valid, hs, w_gate, w_up, w_down_grouped)


def _block_table(seg, n_tokens, ntiles, cap):
    bm = MOE_BLOCK
    nb_max = _round_up(n_tokens + N_GROUPS * SEG_ALIGN * ntiles, bm) // bm + 2 * N_GROUPS
    total = seg[:N_GROUPS, ntiles]
    nb = (total + bm - 1) // bm + 1
    end = jnp.cumsum(nb)
    start = end - nb
    b = jnp.arange(nb_max, dtype=jnp.int32)
    g = jnp.sum((b[:, None] >= end[None, :]).astype(jnp.int32), axis=1)
    used = g < N_GROUPS
    gc = jnp.minimum(g, N_GROUPS - 1)
    j = b - start[gc]
    blk = jnp.where(used, gc * (cap // bm) + j, N_GROUPS * cap // bm - 1)
    compute = used & (j < nb[gc] - 1)
    return blk.astype(jnp.int32), gc.astype(jnp.int32), compute.astype(jnp.int32)


def _unsort_norm_kernel(seg_ref, x1_ref, route_ref, gfin_ref, os_ref, y_ref, win_ref, sem_ref,
                        *, tm, cap, ntiles):
    i = pl.program_id(0)
    slot = lax.rem(i, 2)
    half = tm // 2

    def is_small(tile):
        return seg_ref[N_GROUPS * (ntiles + 1) + tile] == 1

    def windows(tile, slot_, rows):
        out = []
        for g in range(N_GROUPS):
            start = seg_ref[g * (ntiles + 1) + tile]
            out.append(pltpu.make_async_copy(
                os_ref.at[pl.ds(pl.multiple_of(g * cap + start, SEG_ALIGN), rows)],
                win_ref.at[slot_, g, pl.ds(0, rows)], sem_ref.at[slot_]))
        return out

    def for_tile(tile, slot_, action):
        for rows, cond in ((half, is_small(tile)), (tm, jnp.logical_not(is_small(tile)))):
            @pl.when(cond)
            def _():
                for c in windows(tile, slot_, rows):
                    action(c)

    @pl.when(i == 0)
    def _():
        for_tile(0, 0, lambda c: c.start())

    @pl.when(i + 1 < ntiles)
    def _():
        for_tile(i + 1, 1 - slot, lambda c: c.start())

    for_tile(i, slot, lambda c: c.wait())

    where = route_ref[:, ROUTE_LANE:ROUTE_LANE + 1]

    def finish(rows, where_):
        pick = jnp.where(lax.broadcasted_iota(jnp.int32, (tm, N_GROUPS * rows), 1).astype(F32) == where_,
                         1.0, 0.0)
        picked = win_ref[slot, :, 0:rows, :].reshape(N_GROUPS * rows, D_MODEL)
        x2 = x1_ref[...] + jnp.dot(pick.astype(BF16), picked, preferred_element_type=F32)
        ms = jnp.mean(x2 * x2, axis=-1, keepdims=True)
        y_ref[...] = x2 * lax.rsqrt(ms + EPS) * gfin_ref[...]

    @pl.when(is_small(i))
    def _():
        group = jnp.floor(where * (1.0 / tm))
        finish(half, where - group * (tm - half))

    @pl.when(jnp.logical_not(is_small(i)))
    def _():
        finish(tm, where)


def _unsort_norm(seg, x1, route, g_final, out_sorted, *, tm, cap):
    N = x1.shape[0]
    ntiles = N // tm
    grid_spec = pltpu.PrefetchScalarGridSpec(
        num_scalar_prefetch=1,
        grid=(ntiles,),
        in_specs=[pl.BlockSpec((tm, D_MODEL), lambda i, seg: (i, 0)),
                  pl.BlockSpec((tm, LANES), lambda i, seg: (i, 0)),
                  pl.BlockSpec((1, D_MODEL), lambda i, seg: (0, 0)),
                  pl.BlockSpec(memory_space=pl.ANY)],
        out_specs=pl.BlockSpec((tm, D_MODEL), lambda i, seg: (i, 0)),
        scratch_shapes=[pltpu.VMEM((2, N_GROUPS, tm, D_MODEL), BF16),
                        pltpu.SemaphoreType.DMA((2,))],
    )
    return pl.pallas_call(
        functools.partial(_unsort_norm_kernel, tm=tm, cap=cap, ntiles=ntiles),
        grid_spec=grid_spec,
        out_shape=jax.ShapeDtypeStruct((N, D_MODEL), F32),
        compiler_params=pltpu.CompilerParams(
            dimension_semantics=("arbitrary",), vmem_limit_bytes=VMEM_LIMIT),
        name="unsort_norm",
    )(seg.reshape(-1), x1, route, g_final, out_sorted)


def _layer(x, hist16, cache, start_pos, layer, w, g_final, *, tt, tm):
    B, T, _ = x.shape
    li = _lambda_init(layer)
    k, v, qb, kb, vb, pool, state = _inproj(x, hist16, w["g_mix"], w["w_in"], w["w_vt"], w["w_pool"],
                                            w["pool_scale"], tt=tt, start_pos=start_pos,
                                            transposed_v=cache is None)
    if cache is None:
        attn = _attn_prompt(w["lams"], w["g_subln"], qb, kb, vb, tq=KV_BLOCK, heads=ATTN_HEADS_PER_STEP, li=li)
    else:
        attn = _attn_sample(w["lams"], w["g_subln"], qb, cache[0], cache[1], kb, vb, li=li)
    N = B * T
    ntiles = N // tm
    cap = _sorted_capacity(N, ntiles)
    x1, route, seg, hs = _route_sort(x.reshape(N, D_MODEL), pool.reshape(N, POOL_WIDTH),
                                     attn.reshape(N, QK_WIDTH), w["w_out"], w["g_ffn"],
                                     w["wrt"], w["brt"], tm=ROUTE_STEP, sub=tm)
    blk, grp, valid = _block_table(seg, N, ntiles, cap)
    out_sorted = _group_moe(blk, grp, valid, hs, w["w_gate"], w["w_up"], w["w_down"])
    y = _unsort_norm(seg, x1, route, g_final, out_sorted, tm=tm, cap=cap)
    return y.reshape(B, T, D_MODEL), k, v, state


def kernel(x_prompt, x_sample, cache_k, cache_v, state_pool, g_mix, w_in, w_pool, pool_scale, lam_q1, lam_k1, lam_q2, lam_k2, g_subln, w_out, g_ffn, w_group, b_group, w_erouter, b_erouter, w_gate, w_up, w_down, g_final):
    depth = g_mix.shape[0]
    assert depth == 1, "the final norm is fused into the layer, so exactly one layer is supported"
    l = 0
    wr = jnp.concatenate([w_group[l], w_erouter[l]], axis=1).T
    wr = jnp.pad(wr, ((0, LANES - wr.shape[0]), (0, 0)))
    wr_hi = wr.astype(BF16)
    wrt = jnp.concatenate([wr_hi, (wr - wr_hi.astype(F32)).astype(BF16)], axis=0)
    br = jnp.pad(jnp.concatenate([b_group[l], b_erouter[l]]), (0, LANES - N_GROUPS - N_EXPERTS))
    w = dict(
        g_mix=g_mix[l][None], w_in=w_in[l].astype(BF16),
        w_vt=w_in[l][:, POOL_WIDTH + 2 * QK_WIDTH:].T.astype(BF16), w_pool=w_pool[l].astype(BF16),
        pool_scale=pool_scale[l][None],
        lams=(lam_q1[l][None], lam_k1[l][None], lam_q2[l][None], lam_k2[l][None]),
        g_subln=g_subln[l][None], w_out=w_out[l].astype(BF16), g_ffn=g_ffn[l][None],
        wrt=wrt, brt=br[:, None],
        w_gate=w_gate[l].astype(BF16), w_up=w_up[l].astype(BF16),
        w_down=w_down[l].astype(BF16).reshape(N_GROUPS, EXPERTS_PER_GROUP * D_EXPERT, D_MODEL),
    )
    gfin = g_final[None]
    Bp = x_prompt.shape[0]
    Bs, _, _ = x_sample.shape
    P = cache_k.shape[2]
    zero_hist = jnp.zeros((Bp, HIST_ROWS, POOL_WIDTH), F32)
    samp_hist = jnp.pad(state_pool[l], ((0, 0), (HIST_ROWS - POOL_HIST, 0), (0, 0)))
    cache = (cache_k[l], cache_v[l])

    yp, kp, vp, pp = _layer(x_prompt, zero_hist, None, 0, l, w, gfin, tt=512, tm=ROUTE_TILE)
    ys, kn, vn, pn = _layer(x_sample, samp_hist, cache, P, l, w, gfin, tt=64, tm=ROUTE_TILE)
    return (yp, ys, kp[None], vp[None], pp[None], kn[None], vn[None], pn[None])
```

```python
import functools
import math

import jax
import jax.numpy as jnp
from jax import lax
from jax.experimental import pallas as pl
from jax.experimental.pallas import tpu as pltpu

F32 = jnp.float32
BF16 = jnp.bfloat16

D_MODEL = 1024
CHUNK = 64
POOL_WIDTH = 512
POOL_WINDOWS = (2, 4, 8, 16)
POOL_GROUP = 128
POOL_HIST = 15
HIST_ROWS = 16
N_HEADS = 4
HEAD_DIM = 64
HEAD_WIDTH = 128
QK_WIDTH = 512
PROJ_WIDTH = 2048
ATTN_SCALE = HEAD_DIM ** -0.5
LOG2_E = math.log2(math.e)
NEG_INF = -1e30
N_GROUPS = 4
EXPERTS_PER_GROUP = 4
N_EXPERTS = 16
D_EXPERT = 256
EPS = 1e-6
LANES = 128
KV_BLOCK = 256
Q_SUB = 128
ATTN_HEADS_PER_STEP = 2
ROUTER_LANE0 = N_GROUPS
ROUTE_TILE = 256
ROUTE_STEP = 512
ROUTE_LANE = 8
SEG_ALIGN = 16
ROW_W = D_MODEL + LANES
MOE_BLOCK = 512
ZERO_ROWS = 256
ZERO_FILL_ROWS = 2 * MOE_BLOCK
VMEM_LIMIT = 48 * 1024 * 1024


def _lambda_init(layer):
    return 0.8 - 0.6 * math.exp(-0.3 * layer)


def _inproj_kernel(x_ref, hist_ref, gmix_ref, win_ref, wpool_ref, pscale_ref,
                   k_hbm, v_hbm, qb_ref, kb_ref, vb_ref, pool_ref, state_ref,
                   ext_ref, kv_buf, sem_ref, *, tt, start_pos, transposed_v):
    b = pl.program_id(0)
    t = pl.program_id(1)
    step = b * pl.num_programs(1) + t
    last_step = pl.num_programs(0) * pl.num_programs(1) - 1
    slot = lax.rem(step, 2)

    def kv_copies(bb, tile, slot_):
        out = []
        for which, dst in enumerate((k_hbm, v_hbm)):
            for hd in range(N_HEADS):
                out.append(pltpu.make_async_copy(
                    kv_buf.at[slot_, which, :, pl.ds(hd * HEAD_WIDTH, HEAD_WIDTH)],
                    dst.at[bb, pl.ds(pl.multiple_of(tile * tt, tt), tt), hd, :],
                    sem_ref.at[slot_]))
        return out

    @pl.when(step == 0)
    def _():
        ext_ref[...] = jnp.zeros(ext_ref.shape, F32)

    x = x_ref[0]
    ms = jnp.mean(x * x, axis=-1, keepdims=True)
    h = (x * lax.rsqrt(ms + EPS) * gmix_ref[...]).astype(BF16)
    u = jnp.dot(h, win_ref[:, :POOL_WIDTH], preferred_element_type=F32)
    qkv = jnp.dot(h, win_ref[:, POOL_WIDTH:], preferred_element_type=F32)
    q = qkv[:, :QK_WIDTH]
    k = qkv[:, QK_WIDTH:2 * QK_WIDTH]
    v = qkv[:, 2 * QK_WIDTH:]

    qb_ref[0] = (q * (ATTN_SCALE * LOG2_E if transposed_v else ATTN_SCALE)).astype(BF16)
    kb_ref[0] = k.astype(BF16)
    if transposed_v:
        vt = v.T.astype(BF16)
        for i in range(tt // KV_BLOCK):
            vb_ref[0, i] = vt[:, i * KV_BLOCK:(i + 1) * KV_BLOCK]
    else:
        vb_ref[0] = v.astype(BF16)

    ext_ref[0:HIST_ROWS] = jnp.where(t == 0, hist_ref[0], ext_ref[tt:tt + HIST_ROWS])
    ext_ref[HIST_ROWS:HIST_ROWS + tt] = u

    pos = start_pos + t * tt + lax.broadcasted_iota(jnp.int32, (tt, 1), 0)
    outs = []
    for g, w in enumerate(POOL_WINDOWS):
        lo = g * POOL_GROUP
        ug = u[:, lo:lo + POOL_GROUP]
        acc = ug
        for j in range(1, w):
            acc = acc + ext_ref[HIST_ROWS - j:HIST_ROWS - j + tt, lo:lo + POOL_GROUP]
        inv_cnt = 1.0 / jnp.minimum(w, pos + 1).astype(F32)
        d = acc * inv_cnt - ug
        outs.append(jnp.dot(d.astype(BF16), wpool_ref[g], preferred_element_type=F32))
    y = jnp.concatenate(outs, axis=-1) * pscale_ref[...]
    pool_ref[0] = y.astype(BF16)

    @pl.when(t == pl.num_programs(1) - 1)
    def _():
        state_ref[0] = ext_ref[tt + 1:tt + HIST_ROWS]

    @pl.when(step >= 2)
    def _():
        for c in kv_copies(b, t, slot):
            c.wait()

    kv_buf[slot, 0] = k
    kv_buf[slot, 1] = v
    for c in kv_copies(b, t, slot):
        c.start()

    @pl.when(step == last_step)
    def _():
        for c in kv_copies(b, t, 1 - slot) + kv_copies(b, t, slot):
            c.wait()


def _inproj(x, hist16, g_mix, w_in, w_pool, pool_scale, *, tt, start_pos, transposed_v):
    B, T, _ = x.shape
    nt = T // tt
    row = lambda b, t: (b, t, 0)
    const2 = lambda b, t: (0, 0)
    wide = lambda dt: jax.ShapeDtypeStruct((B, T, QK_WIDTH), dt)
    if transposed_v:
        vb_spec = pl.BlockSpec((1, tt // KV_BLOCK, QK_WIDTH, KV_BLOCK), lambda b, t: (b, t, 0, 0))
        vb_shape = jax.ShapeDtypeStruct((B, T // KV_BLOCK, QK_WIDTH, KV_BLOCK), BF16)
    else:
        vb_spec = pl.BlockSpec((1, tt, QK_WIDTH), row)
        vb_shape = wide(BF16)
    return pl.pallas_call(
        functools.partial(_inproj_kernel, tt=tt, start_pos=start_pos, transposed_v=transposed_v),
        grid=(B, nt),
        in_specs=[
            pl.BlockSpec((1, tt, D_MODEL), row),
            pl.BlockSpec((1, HIST_ROWS, POOL_WIDTH), lambda b, t: (b, 0, 0)),
            pl.BlockSpec((1, D_MODEL), const2),
            pl.BlockSpec((D_MODEL, PROJ_WIDTH), const2),
            pl.BlockSpec((len(POOL_WINDOWS), POOL_GROUP, POOL_GROUP), lambda b, t: (0, 0, 0)),
            pl.BlockSpec((1, POOL_WIDTH), const2),
        ],
        out_specs=[
            pl.BlockSpec(memory_space=pl.ANY),
            pl.BlockSpec(memory_space=pl.ANY),
            pl.BlockSpec((1, tt, QK_WIDTH), row),
            pl.BlockSpec((1, tt, QK_WIDTH), row),
            vb_spec,
            pl.BlockSpec((1, tt, POOL_WIDTH), row),
            pl.BlockSpec((1, POOL_HIST, POOL_WIDTH), lambda b, t: (b, 0, 0)),
        ],
        out_shape=[jax.ShapeDtypeStruct((B, T, N_HEADS, HEAD_WIDTH), F32),
                   jax.ShapeDtypeStruct((B, T, N_HEADS, HEAD_WIDTH), F32),
                   wide(BF16), wide(BF16), vb_shape, wide(BF16),
                   jax.ShapeDtypeStruct((B, POOL_HIST, POOL_WIDTH), F32)],
        scratch_shapes=[pltpu.VMEM((HIST_ROWS + tt, POOL_WIDTH), F32),
                        pltpu.VMEM((2, 2, tt, QK_WIDTH), F32),
                        pltpu.SemaphoreType.DMA((2,))],
        compiler_params=pltpu.CompilerParams(
            dimension_semantics=("arbitrary", "arbitrary"), vmem_limit_bytes=VMEM_LIMIT),
        name="inproj_pool",
    )(x, hist16, g_mix, w_in, w_pool, pool_scale)


def _lambda_value(lq1_ref, lk1_ref, lq2_ref, lk2_ref, li):
    s1 = jnp.sum(lq1_ref[...] * lk1_ref[...], axis=-1, keepdims=True)
    s2 = jnp.sum(lq2_ref[...] * lk2_ref[...], axis=-1, keepdims=True)
    return jnp.exp(s1) - jnp.exp(s2) + li


def _split_halves(q):
    lane = lax.broadcasted_iota(jnp.int32, (1, HEAD_WIDTH), 1)
    zero = jnp.zeros_like(q)
    return jnp.where(lane < HEAD_DIM, q, zero), jnp.where(lane >= HEAD_DIM, q, zero)


def _scores(qh, kblk):
    return lax.dot_general(qh, kblk, (((1,), (1,)), ((), ())), preferred_element_type=F32)


def _subln(o, gsub_ref, li):
    return o * lax.rsqrt(jnp.mean(o * o, axis=-1, keepdims=True) + EPS) * gsub_ref[...] * (1.0 - li)


def _attn_prompt_kernel(lq1_ref, lk1_ref, lq2_ref, lk2_ref, gsub_ref, bias_ref, q_ref, k_ref, vt_ref, o_ref,
                        m_ref, l_ref, acc_ref, *, tq, heads, li):
    tk = KV_BLOCK
    n_sub = tq // Q_SUB
    n_tiles = q_ref.shape[1] // tq
    head_cols = [slice(hh * HEAD_WIDTH, (hh + 1) * HEAD_WIDTH) for hh in range(heads)]

    def query_maps(qi):
        q2 = []
        for cols in head_cols:
            for a in range(n_sub):
                rows = q_ref[0, pl.ds(pl.multiple_of(qi * tq + a * Q_SUB, Q_SUB), Q_SUB), cols]
                q0, q1 = _split_halves(rows)
                q2.append(jnp.concatenate([q0, q1], axis=0))
        return q2

    def qk(q2, j, masked):
        out = []
        for hh, cols in enumerate(head_cols):
            kblk = k_ref[0, pl.ds(pl.multiple_of(j * tk, tk), tk), cols]
            for a in range(n_sub):
                s = _scores(kblk, q2[hh * n_sub + a])
                if masked:
                    s = s + bias_ref[a]
                out.append((s, jnp.max(s, axis=0, keepdims=True)))
        return tuple(out)

    ones_rows = jnp.ones((SEG_ALIGN, tk), BF16)

    def consume(j, sc):
        for hh, cols in enumerate(head_cols):
            vtb = jnp.concatenate([vt_ref[0, j, cols, :], ones_rows], axis=0)
            for a in range(n_sub):
                c = hh * n_sub + a
                s, cmax = sc[c]
                m_old = m_ref[c]
                m_new = jnp.maximum(m_old, cmax)
                alpha = jnp.exp2(m_old - m_new)
                p = jnp.exp2(s - m_new)
                pv = jnp.dot(vtb, p.astype(BF16), preferred_element_type=F32)
                l_ref[c] = alpha * l_ref[c] + pv[HEAD_WIDTH:HEAD_WIDTH + 1]
                acc_ref[c] = alpha * acc_ref[c] + pv[:HEAD_WIDTH]
                m_ref[c] = m_new

    lam = _lambda_value(lq1_ref, lk1_ref, lq2_ref, lk2_ref, li)
    scale = gsub_ref[...] * (1.0 - li)

    def tile(qi, sc):
        q2 = query_maps(qi)
        m_ref[...] = jnp.full(m_ref.shape, NEG_INF, F32)
        l_ref[...] = jnp.zeros(l_ref.shape, F32)
        acc_ref[...] = jnp.zeros(acc_ref.shape, F32)

        def body(j, carry):
            prev, cur = carry
            nxt = qk(q2, j, False)
            consume(prev, cur)
            return j, nxt

        last, cur = lax.fori_loop(0, qi, body, (qi, sc))
        qn = jnp.minimum(qi + 1, n_tiles - 1)
        nxt = qk(query_maps(qn), qn, True)
        consume(last, cur)
        for hh, cols in enumerate(head_cols):
            for a in range(n_sub):
                c = hh * n_sub + a
                o2 = acc_ref[c] / l_ref[c]
                o_t = o2[:, :Q_SUB] - lam * o2[:, Q_SUB:]
                o_t = o_t * lax.rsqrt(jnp.mean(o_t * o_t, axis=0, keepdims=True) + EPS)
                rows = pl.ds(pl.multiple_of(qi * tq + a * Q_SUB, Q_SUB), Q_SUB)
                o_ref[0, rows, cols] = (o_t.T * scale).astype(BF16)
        return nxt

    lax.fori_loop(0, n_tiles, tile, qk(query_maps(0), 0, True))


def _attn_prompt(lams, g_subln, qb, kb, vt, *, tq, heads, li):
    B, T, _ = qb.shape
    assert tq == KV_BLOCK and tq % Q_SUB == 0
    key_chunk = jnp.arange(KV_BLOCK)[None, :, None] // CHUNK
    query = jnp.arange(tq // Q_SUB)[:, None, None] * Q_SUB + jnp.arange(2 * Q_SUB)[None, None, :] % Q_SUB
    bias = jnp.where(key_chunk <= query // CHUNK, 0.0, NEG_INF).astype(F32)
    c2 = lambda b, h: (0, 0)
    small = pl.BlockSpec((1, HEAD_DIM), c2)
    width = heads * HEAD_WIDTH
    chains = heads * (tq // Q_SUB)
    head = pl.BlockSpec((1, T, width), lambda b, h: (b, 0, h))
    return pl.pallas_call(
        functools.partial(_attn_prompt_kernel, tq=tq, heads=heads, li=li),
        grid=(B, N_HEADS // heads),
        in_specs=[small, small, small, small,
                  pl.BlockSpec((1, HEAD_WIDTH), c2),
                  pl.BlockSpec(bias.shape, lambda b, h: (0, 0, 0)),
                  head, head,
                  pl.BlockSpec((1, T // KV_BLOCK, width, KV_BLOCK), lambda b, h: (b, 0, h, 0))],
        out_specs=head,
        out_shape=jax.ShapeDtypeStruct((B, T, QK_WIDTH), BF16),
        scratch_shapes=[pltpu.VMEM((chains, 1, 2 * Q_SUB), F32),
                        pltpu.VMEM((chains, 1, 2 * Q_SUB), F32),
                        pltpu.VMEM((chains, HEAD_WIDTH, 2 * Q_SUB), F32)],
        compiler_params=pltpu.CompilerParams(
            dimension_semantics=("parallel", "parallel"), vmem_limit_bytes=VMEM_LIMIT),
        name="attn_prompt",
    )(*lams, g_subln, bias, qb, kb, vt)


def _attn_sample_kernel(lq1_ref, lk1_ref, lq2_ref, lk2_ref, gsub_ref, q_ref, kc_hbm, vc_hbm,
                        kn_ref, vn_ref, o_ref, kc_buf, vc_buf, sem_ref, *, li):
    T = q_ref.shape[1]
    b = pl.program_id(0)
    slot = lax.rem(b, 2)

    def copies(stream, slot_):
        out = []
        for hd in range(N_HEADS):
            out.append(pltpu.make_async_copy(kc_hbm.at[stream, :, hd, :], kc_buf.at[slot_, hd], sem_ref.at[slot_]))
            out.append(pltpu.make_async_copy(vc_hbm.at[stream, :, hd, :], vc_buf.at[slot_, hd], sem_ref.at[slot_]))
        return out

    @pl.when(b == 0)
    def _():
        for c in copies(0, 0):
            c.start()

    @pl.when(b + 1 < pl.num_programs(0))
    def _():
        for c in copies(b + 1, 1 - slot):
            c.start()

    for c in copies(b, slot):
        c.wait()

    lam = _lambda_value(lq1_ref, lk1_ref, lq2_ref, lk2_ref, li)
    for hd in range(N_HEADS):
        cols = slice(hd * HEAD_WIDTH, (hd + 1) * HEAD_WIDTH)
        q2 = jnp.concatenate(_split_halves(q_ref[0, :, cols]), axis=0)
        kc = kc_buf[slot, hd].astype(BF16)
        vc = vc_buf[slot, hd].astype(BF16)
        sc = _scores(q2, kc)
        sn = _scores(q2, kn_ref[0, :, cols])
        m = jnp.maximum(jnp.max(sc, axis=-1, keepdims=True), jnp.max(sn, axis=-1, keepdims=True))
        pc = jnp.exp(sc - m)
        pn = jnp.exp(sn - m)
        l = jnp.sum(pc, axis=-1, keepdims=True) + jnp.sum(pn, axis=-1, keepdims=True)
        acc = (jnp.dot(pc.astype(BF16), vc, preferred_element_type=F32)
               + jnp.dot(pn.astype(BF16), vn_ref[0, :, cols], preferred_element_type=F32))
        o2 = acc / l
        o = o2[:T] - lam * o2[T:]
        o_ref[0, :, cols] = _subln(o, gsub_ref, li).astype(BF16)


def _attn_sample(lams, g_subln, qb, cache_k, cache_v, kb, vb, *, li):
    B, T, _ = qb.shape
    P = cache_k.shape[1]
    c2 = lambda b: (0, 0)
    small = pl.BlockSpec((1, HEAD_DIM), c2)
    rows = pl.BlockSpec((1, T, QK_WIDTH), lambda b: (b, 0, 0))
    in_hbm = pl.BlockSpec(memory_space=pl.ANY)
    return pl.pallas_call(
        functools.partial(_attn_sample_kernel, li=li),
        grid=(B,),
        in_specs=[small, small, small, small, pl.BlockSpec((1, HEAD_WIDTH), c2),
                  rows, in_hbm, in_hbm, rows, rows],
        out_specs=rows,
        out_shape=jax.ShapeDtypeStruct((B, T, QK_WIDTH), BF16),
        scratch_shapes=[pltpu.VMEM((2, N_HEADS, P, HEAD_WIDTH), F32),
                        pltpu.VMEM((2, N_HEADS, P, HEAD_WIDTH), F32),
                        pltpu.SemaphoreType.DMA((2,))],
        compiler_params=pltpu.CompilerParams(
            dimension_semantics=("arbitrary",), vmem_limit_bytes=VMEM_LIMIT),
        name="attn_sample",
    )(*lams, g_subln, qb, cache_k, cache_v, kb, vb)


def _bf16_split(x):
    hi = x.astype(BF16)
    return hi, (x - hi.astype(F32)).astype(BF16)


def _route_sort_kernel(x_ref, pool_ref, attn_ref, wout_ref, gffn_ref, wrt_ref, brt_ref,
                       x1_ref, route_ref, seg_ref, hs_ref,
                       stage_ref, zero_ref, run_ref, sem_ref, *, tm, sub, cap, nsteps):
    i = pl.program_id(0)
    slot = lax.rem(i, 2)
    slots = tm + (tm // sub) * N_GROUPS * SEG_ALIGN

    @pl.when(i == 0)
    def _():
        for g in range(N_GROUPS):
            run_ref[g] = 0
        stage_ref[...] = jnp.zeros(stage_ref.shape, BF16)
        zero_ref[...] = jnp.zeros(zero_ref.shape, BF16)

    mix = (jnp.dot(pool_ref[...], wout_ref[0:POOL_WIDTH, :], preferred_element_type=F32)
           + jnp.dot(attn_ref[...], wout_ref[POOL_WIDTH:, :], preferred_element_type=F32))
    x1 = x_ref[...] + mix
    x1_ref[...] = x1
    ms = jnp.mean(x1 * x1, axis=-1, keepdims=True)
    h2 = x1 * lax.rsqrt(ms + EPS) * gffn_ref[...]
    h_hi, h_lo = _bf16_split(h2)
    nt = (((1,), (1,)), ((), ()))
    hi_lo = lax.dot_general(wrt_ref[...], h_hi, nt, preferred_element_type=F32)
    logits = (hi_lo[:LANES] + hi_lo[LANES:]
              + lax.dot_general(wrt_ref[0:LANES, :], h_lo, nt, preferred_element_type=F32)
              + brt_ref[...])
    row = lambda r: logits[r:r + 1, :]

    def first_argmax(vals):
        best = functools.reduce(jnp.maximum, vals)
        idx = jnp.full(best.shape, len(vals) - 1, jnp.int32)
        for k in range(len(vals) - 2, -1, -1):
            idx = jnp.where(vals[k] == best, k, idx)
        return best, idx

    group_logits = [row(g) for g in range(N_GROUPS)]
    gmax, g_idx = first_argmax(group_logits)
    g_gate = 1.0 / functools.reduce(jnp.add, [jnp.exp(v - gmax) for v in group_logits])
    in_group = [g_idx == g for g in range(N_GROUPS)]
    e_sel = []
    for k in range(EXPERTS_PER_GROUP):
        v = row(ROUTER_LANE0 + (N_GROUPS - 1) * EXPERTS_PER_GROUP + k)
        for g in range(N_GROUPS - 2, -1, -1):
            v = jnp.where(in_group[g], row(ROUTER_LANE0 + g * EXPERTS_PER_GROUP + k), v)
        e_sel.append(v)
    v1, i1 = first_argmax(e_sel)
    v2, i2 = first_argmax([jnp.where(i1 == k, -jnp.inf, e_sel[k]) for k in range(EXPERTS_PER_GROUP)])
    e21 = jnp.exp(v2 - v1)
    w1 = g_gate / (1.0 + e21)
    w2 = w1 * e21
    w1h = w1.astype(BF16).astype(F32)
    w2h = w2.astype(BF16).astype(F32)

    onehot = [jnp.where(m, 1.0, 0.0) for m in in_group]
    onehot8 = jnp.concatenate(onehot + [jnp.zeros((8 - N_GROUPS, tm), F32)], axis=0)
    earlier = lax.broadcasted_iota(jnp.int32, (tm, tm), 0)
    token = lax.broadcasted_iota(jnp.int32, (tm, tm), 1)
    before = (earlier < token) & (earlier // sub == token // sub)
    rank_all = jnp.dot(onehot8.astype(BF16), jnp.where(before, 1.0, 0.0).astype(BF16),
                       preferred_element_type=F32)
    rank = functools.reduce(jnp.add, [onehot[g] * rank_all[g:g + 1] for g in range(N_GROUPS)])
    n_sub = tm // sub
    n_pad = []
    for s in range(n_sub):
        n = jnp.sum(onehot8[:, s * sub:(s + 1) * sub], axis=1, keepdims=True)
        n_pad.append(jnp.floor((n + (SEG_ALIGN - 1)) * (1.0 / SEG_ALIGN)) * SEG_ALIGN)
    sub_of_token = lax.broadcasted_iota(jnp.int32, (1, tm), 1) // sub
    off = {}
    acc = jnp.zeros((1, 1), F32)
    local = rank
    for g in range(N_GROUPS):
        for s in range(n_sub):
            off[g, s] = acc
            local = local + jnp.where(in_group[g] & (sub_of_token == s), acc, 0.0)
            acc = acc + n_pad[s][g:g + 1]

    hi = [jnp.where(i1 == k, w1h, 0.0) + jnp.where(i2 == k, w2h, 0.0) for k in range(EXPERTS_PER_GROUP)]
    lo = [jnp.where(i1 == k, w1 - w1h, 0.0) + jnp.where(i2 == k, w2 - w2h, 0.0)
          for k in range(EXPERTS_PER_GROUP)]
    where = g_idx.astype(F32) * sub + rank
    per_token = jnp.concatenate(hi + lo + [where, jnp.zeros((LANES - ROUTE_LANE - 1, tm), F32)], axis=0).T
    route_ref[...] = per_token

    place = jnp.where(lax.broadcasted_iota(jnp.int32, (slots, tm), 0).astype(F32) == local, 1.0, 0.0)
    rows = jnp.concatenate([h_hi, per_token.astype(BF16)], axis=1)
    sorted_rows = jnp.dot(place.astype(BF16), rows, preferred_element_type=F32)
    stage_ref[slot, 0:slots, :] = sorted_rows.astype(BF16)

    def window(g, slot_, src_row, dst_row):
        return pltpu.make_async_copy(
            stage_ref.at[slot_, pl.ds(pl.multiple_of(src_row, SEG_ALIGN), tm)],
            hs_ref.at[pl.ds(pl.multiple_of(dst_row, SEG_ALIGN), tm)],
            sem_ref.at[slot_])

    @pl.when(i > 0)
    def _():
        for g in range(N_GROUPS):
            window(g, 1 - slot, 0, 0).wait()

    for g in range(N_GROUPS):
        src = off[g, 0][0, 0].astype(jnp.int32)
        start = run_ref[g]
        window(g, slot, src, g * cap + start).start()
        for s in range(n_sub):
            seg_ref[g, n_sub * i + s] = start
            start = start + n_pad[s][g, 0].astype(jnp.int32)
        run_ref[g] = start
    for s in range(n_sub):
        biggest = jnp.max(n_pad[s][0:N_GROUPS], axis=0, keepdims=True)[0, 0]
        seg_ref[N_GROUPS, n_sub * i + s] = (biggest <= sub // 2).astype(jnp.int32)

    @pl.when(i == nsteps - 1)
    def _():
        for g in range(N_GROUPS):
            window(g, slot, 0, 0).wait()
        fills = []
        for g in range(N_GROUPS):
            total = run_ref[g]
            seg_ref[g, n_sub * nsteps] = total
            seg_ref[N_GROUPS, n_sub * nsteps] = 0
            for z in range(ZERO_FILL_ROWS // ZERO_ROWS):
                dst = g * cap + total + z * ZERO_ROWS
                fills.append(pltpu.make_async_copy(
                    zero_ref, hs_ref.at[pl.ds(pl.multiple_of(dst, SEG_ALIGN), ZERO_ROWS)],
                    sem_ref.at[2]))
        for f in fills:
            f.start()
        for f in fills:
            f.wait()


def _round_up(x, m):
    return (x + m - 1) // m * m


def _sorted_capacity(n_tokens, ntiles):
    return _round_up(n_tokens + SEG_ALIGN * ntiles, MOE_BLOCK) + 3 * MOE_BLOCK


def _route_sort(x2d, pool2d, attn2d, w_out, g_ffn, wrt, brt, *, tm, sub):
    N = x2d.shape[0]
    nsteps = N // tm
    ntiles = N // sub
    cap = _sorted_capacity(N, ntiles)
    row = lambda i: (i, 0)
    c2 = lambda i: (0, 0)
    return pl.pallas_call(
        functools.partial(_route_sort_kernel, tm=tm, sub=sub, cap=cap, nsteps=nsteps),
        grid=(nsteps,),
        in_specs=[pl.BlockSpec((tm, D_MODEL), row),
                  pl.BlockSpec((tm, POOL_WIDTH), row),
                  pl.BlockSpec((tm, QK_WIDTH), row),
                  pl.BlockSpec((D_MODEL, D_MODEL), c2),
                  pl.BlockSpec((1, D_MODEL), c2),
                  pl.BlockSpec((2 * LANES, D_MODEL), c2),
                  pl.BlockSpec((LANES, 1), c2)],
        out_specs=[pl.BlockSpec((tm, D_MODEL), row),
                   pl.BlockSpec((tm, LANES), row),
                   pl.BlockSpec(memory_space=pltpu.SMEM),
                   pl.BlockSpec(memory_space=pl.ANY)],
        out_shape=[jax.ShapeDtypeStruct((N, D_MODEL), F32),
                   jax.ShapeDtypeStruct((N, LANES), F32),
                   jax.ShapeDtypeStruct((N_GROUPS + 1, ntiles + 1), jnp.int32),
                   jax.ShapeDtypeStruct((N_GROUPS * cap, ROW_W), BF16)],
        scratch_shapes=[pltpu.VMEM((2, 2 * tm + (tm // sub) * N_GROUPS * SEG_ALIGN, ROW_W), BF16),
                        pltpu.VMEM((ZERO_ROWS, ROW_W), BF16),
                        pltpu.SMEM((N_GROUPS,), jnp.int32),
                        pltpu.SemaphoreType.DMA((3,))],
        compiler_params=pltpu.CompilerParams(
            dimension_semantics=("arbitrary",), vmem_limit_bytes=VMEM_LIMIT),
        name="route_sort",
    )(x2d, pool2d, attn2d, w_out, g_ffn, wrt, brt)


def _group_moe_kernel(blk_ref, grp_ref, valid_ref, hs_ref, wg_ref, wu_ref, wd_ref, out_ref):
    b = pl.program_id(0)

    @pl.when(valid_ref[b] == 1)
    def _():
        rows = hs_ref[...]
        x = rows[:, :D_MODEL]
        cparts = rows[:, D_MODEL:].astype(F32)
        lane = lax.broadcasted_iota(jnp.int32, cparts.shape, 1)
        hes = []
        for e in range(EXPERTS_PER_GROUP):
            a = jnp.dot(x, wg_ref[e], preferred_element_type=F32)
            u = jnp.dot(x, wu_ref[e], preferred_element_type=F32)
            c = jnp.sum(jnp.where((lane == e) | (lane == e + EXPERTS_PER_GROUP), cparts, 0.0),
                        axis=-1, keepdims=True)
            hes.append((a * jax.nn.sigmoid(a) * u * c).astype(BF16))
        he = jnp.concatenate(hes, axis=1)
        out_ref[...] = jnp.dot(he, wd_ref[0], preferred_element_type=F32).astype(BF16)

    @pl.when(valid_ref[b] == 0)
    def _():
        out_ref[...] = jnp.zeros(out_ref.shape, BF16)


def _group_moe(blk, grp, valid, hs, w_gate, w_up, w_down_grouped):
    nb = blk.shape[0]
    bm = MOE_BLOCK
    by_block = lambda b, blk, grp, valid: (blk[b], 0)
    by_group = lambda b, blk, grp, valid: (grp[b], 0, 0)
    grid_spec = pltpu.PrefetchScalarGridSpec(
        num_scalar_prefetch=3,
        grid=(nb,),
        in_specs=[pl.BlockSpec((bm, ROW_W), by_block),
                  pl.BlockSpec((EXPERTS_PER_GROUP, D_MODEL, D_EXPERT), by_group),
                  pl.BlockSpec((EXPERTS_PER_GROUP, D_MODEL, D_EXPERT), by_group),
                  pl.BlockSpec((1, EXPERTS_PER_GROUP * D_EXPERT, D_MODEL), by_group)],
        out_specs=pl.BlockSpec((bm, D_MODEL), by_block),
    )
    return pl.pallas_call(
        _group_moe_kernel,
        grid_spec=grid_spec,
        out_shape=jax.ShapeDtypeStruct((hs.shape[0], D_MODEL), BF16),
        compiler_params=pltpu.CompilerParams(
            dimension_semantics=("arbitrary",), vmem_limit_bytes=VMEM_LIMIT),
        name="group_moe",
    )(blk, grp, valid, hs, w_gate, w_up, w_down_grouped)


def _block_table(seg, n_tokens, ntiles, cap):
    bm = MOE_BLOCK
    nb_max = _round_up(n_tokens + N_GROUPS * SEG_ALIGN * ntiles, bm) // bm + 2 * N_GROUPS
    total = seg[:N_GROUPS, ntiles]
    nb = (total + bm - 1) // bm + 1
    end = jnp.cumsum(nb)
    start = end - nb
    b = jnp.arange(nb_max, dtype=jnp.int32)
    g = jnp.sum((b[:, None] >= end[None, :]).astype(jnp.int32), axis=1)
    used = g < N_GROUPS
    gc = jnp.minimum(g, N_GROUPS - 1)
    j = b - start[gc]
    blk = jnp.where(used, gc * (cap // bm) + j, N_GROUPS * cap // bm - 1)
    compute = used & (j < nb[gc] - 1)
    return blk.astype(jnp.int32), gc.astype(jnp.int32), compute.astype(jnp.int32)


def _unsort_norm_kernel(seg_ref, x1_ref, route_ref, gfin_ref, os_ref, y_ref, win_ref, sem_ref,
                        *, tm, cap, ntiles):
    i = pl.program_id(0)
    slot = lax.rem(i, 2)
    half = tm // 2

    def is_small(tile):
        return seg_ref[N_GROUPS * (ntiles + 1) + tile] == 1

    def windows(tile, slot_, rows):
        out = []
        for g in range(N_GROUPS):
            start = seg_ref[g * (ntiles + 1) + tile]
            out.append(pltpu.make_async_copy(
                os_ref.at[pl.ds(pl.multiple_of(g * cap + start, SEG_ALIGN), rows)],
                win_ref.at[slot_, g, pl.ds(0, rows)], sem_ref.at[slot_]))
        return out

    def for_tile(tile, slot_, action):
        for rows, cond in ((half, is_small(tile)), (tm, jnp.logical_not(is_small(tile)))):
            @pl.when(cond)
            def _():
                for c in windows(tile, slot_, rows):
                    action(c)

    @pl.when(i == 0)
    def _():
        for_tile(0, 0, lambda c: c.start())

    @pl.when(i + 1 < ntiles)
    def _():
        for_tile(i + 1, 1 - slot, lambda c: c.start())

    for_tile(i, slot, lambda c: c.wait())

    where = route_ref[:, ROUTE_LANE:ROUTE_LANE + 1]

    def finish(rows, where_):
        pick = jnp.where(lax.broadcasted_iota(jnp.int32, (tm, N_GROUPS * rows), 1).astype(F32) == where_,
                         1.0, 0.0)
        picked = win_ref[slot, :, 0:rows, :].reshape(N_GROUPS * rows, D_MODEL)
        x2 = x1_ref[...] + jnp.dot(pick.astype(BF16), picked, preferred_element_type=F32)
        ms = jnp.mean(x2 * x2, axis=-1, keepdims=True)
        y_ref[...] = x2 * lax.rsqrt(ms + EPS) * gfin_ref[...]

    @pl.when(is_small(i))
    def _():
        group = jnp.floor(where * (1.0 / tm))
        finish(half, where - group * (tm - half))

    @pl.when(jnp.logical_not(is_small(i)))
    def _():
        finish(tm, where)


def _unsort_norm(seg, x1, route, g_final, out_sorted, *, tm, cap):
    N = x1.shape[0]
    ntiles = N // tm
    grid_spec = pltpu.PrefetchScalarGridSpec(
        num_scalar_prefetch=1,
        grid=(ntiles,),
        in_specs=[pl.BlockSpec((tm, D_MODEL), lambda i, seg: (i, 0)),
                  pl.BlockSpec((tm, LANES), lambda i, seg: (i, 0)),
                  pl.BlockSpec((1, D_MODEL), lambda i, seg: (0, 0)),
                  pl.BlockSpec(memory_space=pl.ANY)],
        out_specs=pl.BlockSpec((tm, D_MODEL), lambda i, seg: (i, 0)),
        scratch_shapes=[pltpu.VMEM((2, N_GROUPS, tm, D_MODEL), BF16),
                        pltpu.SemaphoreType.DMA((2,))],
    )
    return pl.pallas_call(
        functools.partial(_unsort_norm_kernel, tm=tm, cap=cap, ntiles=ntiles),
        grid_spec=grid_spec,
        out_shape=jax.ShapeDtypeStruct((N, D_MODEL), F32),
        compiler_params=pltpu.CompilerParams(
            dimension_semantics=("arbitrary",), vmem_limit_bytes=VMEM_LIMIT),
        name="unsort_norm",
    )(seg.reshape(-1), x1, route, g_final, out_sorted)


def _layer(x, hist16, cache, start_pos, layer, w, g_final, *, tt, tm):
    B, T, _ = x.shape
    li = _lambda_init(layer)
    k, v, qb, kb, vb, pool, state = _inproj(x, hist16, w["g_mix"], w["w_in"], w["w_pool"],
                                            w["pool_scale"], tt=tt, start_pos=start_pos,
                                            transposed_v=cache is None)
    if cache is None:
        attn = _attn_prompt(w["lams"], w["g_subln"], qb, kb, vb, tq=KV_BLOCK, heads=ATTN_HEADS_PER_STEP, li=li)
    else:
        attn = _attn_sample(w["lams"], w["g_subln"], qb, cache[0], cache[1], kb, vb, li=li)
    N = B * T
    ntiles = N // tm
    cap = _sorted_capacity(N, ntiles)
    x1, route, seg, hs = _route_sort(x.reshape(N, D_MODEL), pool.reshape(N, POOL_WIDTH),
                                     attn.reshape(N, QK_WIDTH), w["w_out"], w["g_ffn"],
                                     w["wrt"], w["brt"], tm=ROUTE_STEP, sub=tm)
    blk, grp, valid = _block_table(seg, N, ntiles, cap)
    out_sorted = _group_moe(blk, grp, valid, hs, w["w_gate"], w["w_up"], w["w_down"])
    y = _unsort_norm(seg, x1, route, g_final, out_sorted, tm=tm, cap=cap)
    return y.reshape(B, T, D_MODEL), k, v, state


def kernel(x_prompt, x_sample, cache_k, cache_v, state_pool, g_mix, w_in, w_pool, pool_scale, lam_q1, lam_k1, lam_q2, lam_k2, g_subln, w_out, g_ffn, w_group, b_group, w_erouter, b_erouter, w_gate, w_up, w_down, g_final):
    depth = g_mix.shape[0]
    assert depth == 1, "the final norm is fused into the layer, so exactly one layer is supported"
    l = 0
    wr = jnp.concatenate([w_group[l], w_erouter[l]], axis=1).T
    wr = jnp.pad(wr, ((0, LANES - wr.shape[0]), (0, 0)))
    wr_hi = wr.astype(BF16)
    wrt = jnp.concatenate([wr_hi, (wr - wr_hi.astype(F32)).astype(BF16)], axis=0)
    br = jnp.pad(jnp.concatenate([b_group[l], b_erouter[l]]), (0, LANES - N_GROUPS - N_EXPERTS))
    w = dict(
        g_mix=g_mix[l][None], w_in=w_in[l].astype(BF16),
        w_pool=w_pool[l].astype(BF16),
        pool_scale=pool_scale[l][None],
        lams=(lam_q1[l][None], lam_k1[l][None], lam_q2[l][None], lam_k2[l][None]),
        g_subln=g_subln[l][None], w_out=w_out[l].astype(BF16), g_ffn=g_ffn[l][None],
        wrt=wrt, brt=br[:, None],
        w_gate=w_gate[l].astype(BF16), w_up=w_up[l].astype(BF16),
        w_down=w_down[l].astype(BF16).reshape(N_GROUPS, EXPERTS_PER_GROUP * D_EXPERT, D_MODEL),
    )
    gfin = g_final[None]
    Bp = x_prompt.shape[0]
    Bs, _, _ = x_sample.shape
    P = cache_k.shape[2]
    zero_hist = jnp.zeros((Bp, HIST_ROWS, POOL_WIDTH), F32)
    samp_hist = jnp.pad(state_pool[l], ((0, 0), (HIST_ROWS - POOL_HIST, 0), (0, 0)))
    cache = (cache_k[l], cache_v[l])

    yp, kp, vp, pp = _layer(x_prompt, zero_hist, None, 0, l, w, gfin, tt=512, tm=ROUTE_TILE)
    ys, kn, vn, pn = _layer(x_sample, samp_hist, cache, P, l, w, gfin, tt=64, tm=ROUTE_TILE)
    return (yp, ys, kp[None], vp[None], pp[None], kn[None], vn[None], pn[None])
```
